```python
import math
import jax, jax.numpy as jnp
from jax import lax
import numpy as np

D_MODEL = 2048
BATCH = 2
SEQ = 8192
DEPTH = 4

PLE_DIM = 256
HEAD_DIM = 128
NSA_HEADS = D_MODEL // 256
NSA_KV_GROUPS = 2
NSA_HPG = NSA_HEADS // NSA_KV_GROUPS
NSA_WIDTH = NSA_HEADS * HEAD_DIM
KV_W = NSA_KV_GROUPS * HEAD_DIM
CMP_LEN = 32
CMP_STRIDE = 16
SEL_BLOCK = 64
SEL_TOPK = 16
WINDOW = 512
Q_BLOCK = 128
HG_HEADS = D_MODEL // 256
HG_DK = 128
HG_DV = 128
HG_WIDTH = HG_HEADS * HG_DV
HG_CHUNK = 64
CONV_CH = D_MODEL
CONV_K = 31
N_EVEN = (DEPTH + 1) // 2
N_ODD = DEPTH // 2
DEEPNORM_ALPHA = (2.0 * DEPTH) ** 0.25
DEEPNORM_BETA = (8.0 * DEPTH) ** -0.25
LN_EPS = 1e-5
EVEN_SPLITS = (NSA_WIDTH, KV_W, KV_W, KV_W, KV_W, KV_W, KV_W, NSA_HEADS * 3, NSA_WIDTH,
               HG_HEADS * HG_DK, HG_HEADS * HG_DK, HG_WIDTH, HG_WIDTH)
EVEN_IN = sum(EVEN_SPLITS)

kernel_name = "hybrid_nsa_hgrn2_conformer_deepnorm"


def _layernorm(x, g, b):
    xf = x.astype(jnp.float32)
    mu = jnp.mean(xf, axis=-1, keepdims=True)
    var = jnp.mean(jnp.square(xf - mu), axis=-1, keepdims=True)
    return ((xf - mu) * lax.rsqrt(var + LN_EPS) * g + b).astype(x.dtype)


def _masked_softmax(s, mask):
    s = jnp.where(mask, s.astype(jnp.float32), -1e30)
    return jnp.where(mask, jax.nn.softmax(s, axis=-1), 0.0)


def _alibi_slopes(n):
    return jnp.asarray(2.0 ** (-8.0 * np.arange(1, n + 1) / n), jnp.float32)


def _cmp_sel_overlap(n_cmp, n_sel):
    s = np.arange(n_cmp)[:, None] * CMP_STRIDE
    b = np.arange(n_sel)[None, :] * SEL_BLOCK
    ov = np.clip(np.minimum(s + CMP_LEN, b + SEL_BLOCK) - np.maximum(s, b), 0, None) / CMP_LEN
    return jnp.asarray(ov, jnp.float32)


def _nsa(q, k_cmp, v_cmp, k_slc, v_slc, k_win, v_win, gates, pe_k, w1_k, w2_k, pe_v, w1_v, w2_v):
    B, T, _ = q.shape
    G, Hg, dh = NSA_KV_GROUPS, NSA_HPG, HEAD_DIM
    q = q.reshape(B, T, G, Hg, dh) * dh ** -0.5

    def compress(raw, pe, w1, w2):
        r = raw.reshape(B, T // CMP_STRIDE, CMP_STRIDE, G, dh)
        blk = jnp.concatenate([r[:, :-1], r[:, 1:]], axis=2) + pe[:, None, :]
        blk = blk.transpose(0, 1, 3, 2, 4).reshape(B, -1, G, CMP_LEN * dh)
        return jax.nn.gelu(blk @ w1) @ w2

    kc = compress(k_cmp, pe_k, w1_k, w2_k)
    vc = compress(v_cmp, pe_v, w1_v, w2_v)
    n_cmp = T // CMP_STRIDE - 1
    cmp_end = jnp.arange(n_cmp) * CMP_STRIDE + CMP_LEN - 1
    n_sel = T // SEL_BLOCK
    topk = min(SEL_TOPK, n_sel)
    overlap = _cmp_sel_overlap(n_cmp, n_sel)
    ks = k_slc.reshape(B, n_sel, SEL_BLOCK, G, dh).transpose(0, 3, 1, 2, 4)
    vs = v_slc.reshape(B, n_sel, SEL_BLOCK, G, dh).transpose(0, 3, 1, 2, 4)
    pad = ((0, 0), (WINDOW, 0), (0, 0), (0, 0))
    kw = jnp.pad(k_win.reshape(B, T, G, dh), pad)
    vw = jnp.pad(v_win.reshape(B, T, G, dh), pad)
    gates = jax.nn.sigmoid(gates.astype(jnp.float32)).reshape(B, T, G, Hg, 3)
    slopes = _alibi_slopes(NSA_HEADS).reshape(1, G, Hg, 1, 1)
    bidx = jnp.arange(B)[:, None, None, None]
    gidx = jnp.arange(G)[None, :, None, None]
    sel_j = jnp.arange(n_sel)
    m_sel = topk * SEL_BLOCK

    def block(bi):
        t0 = bi * Q_BLOCK
        tpos = t0 + jnp.arange(Q_BLOCK)
        qb = lax.dynamic_slice_in_dim(q, t0, Q_BLOCK, axis=1)
        d_c = tpos[:, None] - cmp_end[None, :]
        s = jnp.einsum('btghd,bngd->bghtn', qb, kc) - slopes * d_c
        p_c = _masked_softmax(s, d_c >= 0)
        o_c = jnp.einsum('bghtn,bngd->btghd', p_c, vc)
        imp = jnp.einsum('bghtn,nj->bgtj', p_c, overlap)
        cur = (tpos // SEL_BLOCK)[:, None]
        forced = (sel_j == 0) | (sel_j == cur) | (sel_j == cur - 1)
        imp = jnp.where(forced, 1e9, imp)
        imp = jnp.where(sel_j * SEL_BLOCK > tpos[:, None], -1e9, imp)
        _, idx = lax.top_k(imp, topk)
        kg = ks[bidx, gidx, idx].reshape(B, G, Q_BLOCK, m_sel, dh)
        vg = vs[bidx, gidx, idx].reshape(B, G, Q_BLOCK, m_sel, dh)
        pos = (idx[..., None] * SEL_BLOCK + jnp.arange(SEL_BLOCK)).reshape(B, G, 1, Q_BLOCK, m_sel)
        d_s = tpos[:, None] - pos
        s = jnp.einsum('btghd,bgtmd->bghtm', qb, kg) - slopes * d_s
        p_s = _masked_softmax(s, d_s >= 0)
        o_s = jnp.einsum('bghtm,bgtmd->btghd', p_s, vg)
        kwb = lax.dynamic_slice_in_dim(kw, t0, Q_BLOCK + WINDOW, axis=1)
        vwb = lax.dynamic_slice_in_dim(vw, t0, Q_BLOCK + WINDOW, axis=1)
        kpos = t0 - WINDOW + jnp.arange(Q_BLOCK + WINDOW)
        d_w = tpos[:, None] - kpos[None, :]
        mask_w = (d_w >= 0) & (d_w < WINDOW) & (kpos >= 0)[None, :]
        s = jnp.einsum('btghd,bsgd->bghts', qb, kwb) - slopes * d_w
        p_w = _masked_softmax(s, mask_w)
        o_w = jnp.einsum('bghts,bsgd->btghd', p_w, vwb)
        gb = lax.dynamic_slice_in_dim(gates, t0, Q_BLOCK, axis=1)
        return gb[..., 0:1] * o_c + gb[..., 1:2] * o_s + gb[..., 2:3] * o_w

    out = lax.map(block, jnp.arange(T // Q_BLOCK))
    return out.transpose(1, 0, 2, 3, 4, 5).reshape(B, T, NSA_WIDTH)


def _hgrn2(q, k, logf, v):
    B, T, H, dk = q.shape
    dv = v.shape[-1]
    n = T // HG_CHUNK

    def to_chunks(a):
        return a.reshape(B, n, HG_CHUNK, H, a.shape[-1]).transpose(1, 0, 3, 2, 4)

    qc, kc, gc, vc = to_chunks(q), to_chunks(k), to_chunks(logf), to_chunks(v)
    gc = jnp.cumsum(gc, axis=3)
    mask = jnp.tril(jnp.ones((HG_CHUNK, HG_CHUNK), bool))[:, :, None]

    def step(S, inp):
        q_, k_, g_, v_ = inp
        o_inter = jnp.einsum('bhtk,bhkv->bhtv', q_ * jnp.exp(g_), S)
        diff = g_[:, :, :, None, :] - g_[:, :, None, :, :]
        decay = jnp.where(mask, jnp.exp(jnp.where(mask, diff, 0.0)), 0.0)
        a = jnp.einsum('bhtk,bhtsk,bhsk->bhts', q_, decay, k_)
        o = o_inter + jnp.einsum('bhts,bhsv->bhtv', a, v_)
        g_last = g_[:, :, -1, :]
        S = jnp.exp(g_last)[..., None] * S + jnp.einsum(
            'bhsk,bhsv->bhkv', k_ * jnp.exp(g_last[:, :, None, :] - g_), v_)
        return S, o

    S0 = jnp.zeros((B, H, dk, dv), jnp.float32)
    _, o = lax.scan(step, S0, (qc, kc, gc, vc))
    return o.transpose(1, 0, 3, 2, 4).reshape(B, T, H, dv)


def _even_layer(x, w_in, pe_k, w1_k, w2_k, pe_v, w1_v, w2_v, hg_norm, lb, w_out):
    B, T, _ = x.shape
    parts = jnp.split(x @ w_in, np.cumsum(EVEN_SPLITS)[:-1].tolist(), axis=-1)
    nq, kc, vc, ks, vs, kw, vw, ng, nz, hq, hf, hi, hz = parts
    a = _nsa(nq, kc, vc, ks, vs, kw, vw, ng, pe_k, w1_k, w2_k, pe_v, w1_v, w2_v)
    hq = jax.nn.silu(hq.astype(jnp.float32)).reshape(B, T, HG_HEADS, HG_DK)
    hf = hf.astype(jnp.float32).reshape(B, T, HG_HEADS, HG_DK)
    lbh = lb.reshape(HG_HEADS, HG_DK)
    f = lbh + (1.0 - lbh) * jax.nn.sigmoid(hf)
    k = (1.0 - lbh) * jax.nn.sigmoid(-hf)
    o = _hgrn2(hq, k, jnp.log(f), hi.astype(jnp.float32).reshape(B, T, HG_HEADS, HG_DV))
    o = o * lax.rsqrt(jnp.mean(jnp.square(o), axis=-1, keepdims=True) + LN_EPS)
    o = o.reshape(B, T, HG_WIDTH) * hg_norm
    y = jnp.concatenate([a * jax.nn.silu(nz), o * jax.nn.silu(hz)], axis=-1).astype(x.dtype)
    return y @ w_out


def _conv_layer(x, w_in, conv_w, conv_b, ln_g, ln_b, w_out):
    a, b, z = jnp.split(x @ w_in, 3, axis=-1)
    u = a * jax.nn.sigmoid(b)
    c = lax.conv_general_dilated(u, conv_w[:, None, :].astype(u.dtype), window_strides=(1,),
                                 padding=[(CONV_K - 1, 0)], dimension_numbers=('NWC', 'WIO', 'NWC'),
                                 feature_group_count=CONV_CH) + conv_b
    c = _layernorm(c, ln_g, ln_b)
    return (jax.nn.silu(c) * jax.nn.silu(z)) @ w_out


def setup_inputs(seed: int = 0) -> dict:
    key = jax.random.key(seed)
    ks = jax.random.split(key, 24)

    def nrm(k, shape, scale):
        return jax.random.normal(k, shape, jnp.float32) * scale

    cw = CMP_LEN * HEAD_DIM
    return {
        'x': nrm(ks[0], (BATCH, SEQ, D_MODEL), 1.0),
        'p': nrm(ks[1], (DEPTH, BATCH, SEQ, PLE_DIM), 1.0),
        'ev_w_in': nrm(ks[2], (N_EVEN, D_MODEL, EVEN_IN), D_MODEL ** -0.5),
        'ev_cmp_pe_k': nrm(ks[3], (N_EVEN, CMP_LEN, HEAD_DIM), 0.1),
        'ev_cmp_w1_k': nrm(ks[4], (N_EVEN, cw, HEAD_DIM), cw ** -0.5),
        'ev_cmp_w2_k': nrm(ks[5], (N_EVEN, HEAD_DIM, HEAD_DIM), HEAD_DIM ** -0.5),
        'ev_cmp_pe_v': nrm(ks[6], (N_EVEN, CMP_LEN, HEAD_DIM), 0.1),
        'ev_cmp_w1_v': nrm(ks[7], (N_EVEN, cw, HEAD_DIM), cw ** -0.5),
        'ev_cmp_w2_v': nrm(ks[8], (N_EVEN, HEAD_DIM, HEAD_DIM), HEAD_DIM ** -0.5),
        'ev_hg_norm': 1.0 + nrm(ks[9], (N_EVEN, HG_WIDTH), 0.1),
        'hgrn_lb': nrm(ks[10], (N_EVEN, HG_HEADS * HG_DK), 1.0),
        'ev_w_out': nrm(ks[11], (N_EVEN, NSA_WIDTH + HG_WIDTH, D_MODEL),
                        (NSA_WIDTH + HG_WIDTH) ** -0.5 * DEEPNORM_BETA),
        'od_w_in': nrm(ks[12], (N_ODD, D_MODEL, 3 * CONV_CH), D_MODEL ** -0.5),
        'od_conv_w': nrm(ks[13], (N_ODD, CONV_K, CONV_CH), CONV_K ** -0.5),
        'od_conv_b': nrm(ks[14], (N_ODD, CONV_CH), 0.02),
        'od_ln_g': 1.0 + nrm(ks[15], (N_ODD, CONV_CH), 0.1),
        'od_ln_b': nrm(ks[16], (N_ODD, CONV_CH), 0.02),
        'od_w_out': nrm(ks[17], (N_ODD, CONV_CH, D_MODEL), CONV_CH ** -0.5 * DEEPNORM_BETA),
        'post_ln_g': 1.0 + nrm(ks[18], (DEPTH, D_MODEL), 0.1),
        'post_ln_b': nrm(ks[19], (DEPTH, D_MODEL), 0.02),
        'ple_w': nrm(ks[20], (DEPTH, PLE_DIM, D_MODEL), PLE_DIM ** -0.5),
        'ple_gate_w': nrm(ks[21], (DEPTH, D_MODEL, D_MODEL), D_MODEL ** -0.5),
    }


def reference(x, p, ev_w_in, ev_cmp_pe_k, ev_cmp_w1_k, ev_cmp_w2_k, ev_cmp_pe_v, ev_cmp_w1_v,
              ev_cmp_w2_v, ev_hg_norm, hgrn_lb, ev_w_out, od_w_in, od_conv_w, od_conv_b, od_ln_g,
              od_ln_b, od_w_out, post_ln_g, post_ln_b, ple_w, ple_gate_w):
    lb_all = jnp.cumsum(jax.nn.softmax(hgrn_lb.astype(jnp.float32), axis=0), axis=0)
    lb_all = lb_all - lb_all[0:1]
    for i in range(DEPTH):
        j = i // 2
        if i % 2 == 0:
            y = _even_layer(x, ev_w_in[j], ev_cmp_pe_k[j], ev_cmp_w1_k[j], ev_cmp_w2_k[j],
                            ev_cmp_pe_v[j], ev_cmp_w1_v[j], ev_cmp_w2_v[j], ev_hg_norm[j],
                            lb_all[j], ev_w_out[j])
        else:
            y = _conv_layer(x, od_w_in[j], od_conv_w[j], od_conv_b[j], od_ln_g[j], od_ln_b[j],
                            od_w_out[j])
        x = _layernorm(DEEPNORM_ALPHA * x + y, post_ln_g[i], post_ln_b[i])
        x = x + (p[i] @ ple_w[i]) * jax.nn.sigmoid(x @ ple_gate_w[i])
    return x
```

```python
import functools

import numpy as np
import jax
import jax.numpy as jnp
from jax import lax
from jax.experimental import pallas as pl
from jax.experimental.pallas import tpu as pltpu

F32 = jnp.float32
BF16 = jnp.bfloat16

D_MODEL = 2048
DEPTH = 4
PLE_DIM = 256
HEAD_DIM = 128
NSA_HEADS = 8
NSA_KV_GROUPS = 2
NSA_HPG = NSA_HEADS // NSA_KV_GROUPS
NSA_WIDTH = NSA_HEADS * HEAD_DIM
KV_W = NSA_KV_GROUPS * HEAD_DIM
CMP_LEN = 32
CMP_STRIDE = 16
SEL_BLOCK = 64
SEL_TOPK = 16
WINDOW = 512
Q_BLOCK = 128
HG_HEADS = 8
HG_DK = 128
HG_DV = 128
HG_WIDTH = HG_HEADS * HG_DV
CONV_CH = D_MODEL
CONV_K = 31
N_EVEN = (DEPTH + 1) // 2
DEEPNORM_ALPHA = (2.0 * DEPTH) ** 0.25
LN_EPS = 1e-5

NEG = -1e30
SEL_KEY_TILE = 512
HG_CHUNK = 128
HG_SUB = 16
CONV_ROWS = 128
CONV_HALO = 32
CONV_STRIP = 256
VMEM_LIMIT = 56 * 1024 * 1024


def _params(*sem):
    return pltpu.CompilerParams(dimension_semantics=sem, vmem_limit_bytes=VMEM_LIMIT)


def _dot(a, b):
    return jnp.dot(a, b, preferred_element_type=F32)


def _dot_nt(a, b):
    return lax.dot_general(a, b, (((1,), (1,)), ((), ())), preferred_element_type=F32)


def _dot_tn(a, b):
    return lax.dot_general(a, b, (((0,), (0,)), ((), ())), preferred_element_type=F32)


def _mm_body(x_ref, w_ref, o_ref):
    o_ref[...] = _dot(x_ref[...].astype(BF16), w_ref[...]).astype(o_ref.dtype)


def _matmul(x, w, out_dtype, bm, bn, name):
    m, k = x.shape
    n = w.shape[1]
    return pl.pallas_call(
        _mm_body,
        grid=(m // bm, n // bn),
        in_specs=[pl.BlockSpec((bm, k), lambda i, j: (i, 0)),
                  pl.BlockSpec((k, bn), lambda i, j: (0, j))],
        out_specs=pl.BlockSpec((bm, bn), lambda i, j: (i, j)),
        out_shape=jax.ShapeDtypeStruct((m, n), out_dtype),
        compiler_params=_params("parallel", "arbitrary"),
        name=name,
    )(x, w)


def _cmp_body(r_ref, pe_ref, w1_ref, w2_ref, o_ref, *, n):
    row = lax.broadcasted_iota(jnp.int32, (n, 1), 0)
    for kv in range(2):
        pe8 = jnp.broadcast_to(pe_ref[kv], (8, CMP_LEN * HEAD_DIM)).astype(BF16)
        pe_term = _dot(pe8, w1_ref[kv])[0:1, :]
        for g in range(NSA_KV_GROUPS):
            first = jnp.zeros((n, HEAD_DIM), F32)
            second = jnp.zeros((n, HEAD_DIM), F32)
            for r in range(CMP_STRIDE):
                col = r * 2 * KV_W + kv * KV_W + g * HEAD_DIM
                xr = r_ref[:, col:col + HEAD_DIM]
                first = first + _dot(xr, w1_ref[kv, r * HEAD_DIM:(r + 1) * HEAD_DIM, :])
                lo = (CMP_STRIDE + r) * HEAD_DIM
                second = second + _dot(xr, w1_ref[kv, lo:lo + HEAD_DIM, :])
            pre = first + pltpu.roll(second, n - 1, 0) + pe_term
            out = _dot(jax.nn.gelu(pre).astype(BF16), w2_ref[kv])
            o_ref[kv, g] = jnp.where(row < n - 1, out, 0.0).astype(o_ref.dtype)


def _compress(pc, pe, w1, w2):
    b, t, _ = pc.shape
    n = t // CMP_STRIDE
    r = pc.reshape(b, n, CMP_STRIDE * 2 * KV_W)
    return pl.pallas_call(
        functools.partial(_cmp_body, n=n),
        grid=(b,),
        in_specs=[pl.BlockSpec((None, n, CMP_STRIDE * 2 * KV_W), lambda i: (i, 0, 0)),
                  pl.BlockSpec((2, 1, CMP_LEN * HEAD_DIM), lambda i: (0, 0, 0)),
                  pl.BlockSpec((2, CMP_LEN * HEAD_DIM, HEAD_DIM), lambda i: (0, 0, 0)),
                  pl.BlockSpec((2, HEAD_DIM, HEAD_DIM), lambda i: (0, 0, 0))],
        out_specs=pl.BlockSpec((None, 2, NSA_KV_GROUPS, n, HEAD_DIM), lambda i: (i, 0, 0, 0, 0)),
        out_shape=jax.ShapeDtypeStruct((b, 2, NSA_KV_GROUPS, n, HEAD_DIM), BF16),
        compiler_params=_params("parallel"),
        name="nsa_compress",
    )(r, pe, w1, w2)


def _masked_softmax(s, mask):
    m = jnp.max(s, axis=-1, keepdims=True)
    e = jnp.where(mask, jnp.exp(s - m), 0.0)
    l = jnp.sum(e, axis=-1, keepdims=True)
    return e / jnp.where(l > 0.0, l, 1.0)


def _nsa_body(q_ref, ks_ref, vs_ref, kw_ref, vw_ref, kc_ref, vc_ref, gate_ref, nz_ref, ov_ref,
              eh_ref, o_ref, m_ref, l_ref, acc_ref, ar_ref, *, t_len):
    g = pl.program_id(1)
    i = pl.program_id(2)
    t0 = i * Q_BLOCK
    rows = NSA_HPG * Q_BLOCK
    n_cmp = t_len // CMP_STRIDE
    n_sel = t_len // SEL_BLOCK
    kt = SEL_KEY_TILE

    q4 = jnp.concatenate([q_ref[:, h * HEAD_DIM:(h + 1) * HEAD_DIM] for h in range(NSA_HPG)], axis=0)
    rowi = lax.broadcasted_iota(jnp.int32, (rows, 1), 0)
    hrow = jnp.right_shift(rowi, 7)
    tcol = t0 + (rowi - hrow * Q_BLOCK)
    slope = lax.bitcast_convert_type(jnp.left_shift(127 - (g * NSA_HPG + hrow + 1), 23), F32)

    nidx = lax.broadcasted_iota(jnp.int32, (1, n_cmp), 1)
    d_c = tcol - (nidx * CMP_STRIDE + CMP_LEN - 1)
    mask_c = d_c >= 0
    s = jnp.where(mask_c, _dot_nt(q4, kc_ref[...]) - slope * d_c.astype(F32), NEG)
    p_c = _masked_softmax(s, mask_c)
    o_c = _dot(p_c.astype(BF16), vc_ref[...])
    p_sum = p_c[0:Q_BLOCK]
    for h in range(1, NSA_HPG):
        p_sum = p_sum + p_c[h * Q_BLOCK:(h + 1) * Q_BLOCK]
    p_hi = p_sum.astype(BF16)
    p_lo = (p_sum - p_hi.astype(F32)).astype(BF16)
    imp = _dot(p_hi, ov_ref[...]) + _dot(p_lo, ov_ref[...])

    jidx = lax.broadcasted_iota(jnp.int32, (1, n_sel), 1)
    jf = jidx.astype(F32)
    tt = t0 + lax.broadcasted_iota(jnp.int32, (Q_BLOCK, 1), 0)
    cur = jnp.right_shift(tt, 6)
    forced = (jidx == 0) | (jidx == cur) | (jidx == cur - 1)
    future = jidx * SEL_BLOCK > tt
    work = jnp.where(future, -1e9, jnp.where(forced, 1e9, imp))
    picked = jnp.zeros((Q_BLOCK, n_sel), F32)
    for _ in range(min(SEL_TOPK, n_sel)):
        mx = jnp.max(work, axis=-1, keepdims=True)
        first = jnp.min(jnp.where(work == mx, jf, 1e9), axis=-1, keepdims=True)
        pick = jf == first
        picked = jnp.where(pick, 1.0, picked)
        work = jnp.where(pick, -3e38, work)
    sel = jnp.where(future, 0.0, picked) > 0.5

    rel = ((t0 // SEL_BLOCK) - jidx).astype(F32) * float(SEL_BLOCK)
    q_parts = []
    for h in range(NSA_HPG):
        slope_h = slope[h * Q_BLOCK:(h + 1) * Q_BLOCK]
        bias_h = jnp.where(sel, -(slope_h * rel), NEG).astype(BF16)
        q_parts.append(jnp.concatenate([q4[h * Q_BLOCK:(h + 1) * Q_BLOCK], bias_h], axis=1))
    q_aug = jnp.concatenate(q_parts, axis=0)
    in_block = jnp.bitwise_and(lax.broadcasted_iota(jnp.int32, (1, kt), 1), SEL_BLOCK - 1)
    ar_ref[...] = slope * in_block.astype(F32)
    m_ref[...] = jnp.full((rows, 1), NEG, F32)
    l_ref[...] = jnp.zeros((rows, 1), F32)
    acc_ref[...] = jnp.zeros((rows, HEAD_DIM), F32)

    def sel_tile(c, causal):
        k0 = pl.multiple_of(c * kt, kt)
        k_aug = jnp.concatenate([ks_ref[pl.ds(k0, kt), :], eh_ref[pl.ds(k0, kt), :]], axis=1)
        sc = _dot_nt(q_aug, k_aug) + ar_ref[...]
        if causal:
            kpos = k0 + lax.broadcasted_iota(jnp.int32, (1, kt), 1)
            sc = jnp.where(kpos <= tcol, sc, NEG)
        m_old = m_ref[...]
        m_new = jnp.maximum(m_old, jnp.max(sc, axis=-1, keepdims=True))
        alpha = jnp.exp(m_old - m_new)
        p = jnp.exp(sc - m_new)
        l_ref[...] = alpha * l_ref[...] + jnp.sum(p, axis=-1, keepdims=True)
        acc_ref[...] = alpha * acc_ref[...] + _dot(p.astype(BF16), vs_ref[pl.ds(k0, kt), :])
        m_ref[...] = m_new

    c_last = t0 // kt

    def past_tile(c, carry):
        sel_tile(c, False)
        return carry

    lax.fori_loop(0, c_last, past_tile, 0)
    sel_tile(c_last, True)
    o_s = acc_ref[...] / l_ref[...]

    span = WINDOW + Q_BLOCK
    start = pl.multiple_of(jnp.maximum(t0 - WINDOW, 0), Q_BLOCK)
    kpos = start + lax.broadcasted_iota(jnp.int32, (1, span), 1)
    d_w = tcol - kpos
    mask_w = (d_w >= 0) & (d_w < WINDOW)
    s = jnp.where(mask_w, _dot_nt(q4, kw_ref[pl.ds(start, span), :]) - slope * d_w.astype(F32), NEG)
    p_w = _masked_softmax(s, mask_w)
    o_w = _dot(p_w.astype(BF16), vw_ref[pl.ds(start, span), :])

    gt = jax.nn.sigmoid(gate_ref[...])
    outs = []
    for h in range(NSA_HPG):
        r0, r1 = h * Q_BLOCK, (h + 1) * Q_BLOCK
        a = (gt[:, 3 * h:3 * h + 1] * o_c[r0:r1] + gt[:, 3 * h + 1:3 * h + 2] * o_s[r0:r1]
             + gt[:, 3 * h + 2:3 * h + 3] * o_w[r0:r1])
        outs.append(a * jax.nn.silu(nz_ref[:, h * HEAD_DIM:(h + 1) * HEAD_DIM]))
    o_ref[...] = jnp.concatenate(outs, axis=1).astype(o_ref.dtype)


def _nsa_constants(t_len):
    n_cmp = t_len // CMP_STRIDE
    n_sel = t_len // SEL_BLOCK
    s = np.arange(n_cmp)[:, None] * CMP_STRIDE
    b = np.arange(n_sel)[None, :] * SEL_BLOCK
    ov = np.clip(np.minimum(s + CMP_LEN, b + SEL_BLOCK) - np.maximum(s, b), 0, None) / CMP_LEN
    ov[n_cmp - 1] = 0.0
    onehot = (np.arange(t_len)[:, None] // SEL_BLOCK == np.arange(n_sel)[None, :])
    return jnp.asarray(ov, BF16), jnp.asarray(onehot, BF16)


def _nsa(p16, cmp_kv, gates, p32):
    b, t, _ = p16.shape
    n_cmp = t // CMP_STRIDE
    n_sel = t // SEL_BLOCK
    rows = NSA_HPG * Q_BLOCK
    ov, onehot = _nsa_constants(t)
    qw = NSA_HPG * HEAD_DIM
    kv_base = NSA_WIDTH // HEAD_DIM

    def kv_spec(off):
        return pl.BlockSpec((None, t, HEAD_DIM), lambda bi, g, i: (bi, 0, kv_base + off + g))

    return pl.pallas_call(
        functools.partial(_nsa_body, t_len=t),
        grid=(b, NSA_KV_GROUPS, t // Q_BLOCK),
        in_specs=[
            pl.BlockSpec((None, Q_BLOCK, qw), lambda bi, g, i: (bi, i, g)),
            kv_spec(0), kv_spec(2), kv_spec(4), kv_spec(6),
            pl.BlockSpec((None, None, None, n_cmp, HEAD_DIM), lambda bi, g, i: (bi, 0, g, 0, 0)),
            pl.BlockSpec((None, None, None, n_cmp, HEAD_DIM), lambda bi, g, i: (bi, 1, g, 0, 0)),
            pl.BlockSpec((None, Q_BLOCK, HEAD_DIM), lambda bi, g, i: (bi, i, g)),
            pl.BlockSpec((None, Q_BLOCK, qw), lambda bi, g, i: (bi, i, g)),
            pl.BlockSpec((n_cmp, n_sel), lambda bi, g, i: (0, 0)),
            pl.BlockSpec((t, n_sel), lambda bi, g, i: (0, 0)),
        ],
        out_specs=pl.BlockSpec((None, Q_BLOCK, qw), lambda bi, g, i: (bi, i, g)),
        out_shape=jax.ShapeDtypeStruct((b, t, NSA_WIDTH), BF16),
        scratch_shapes=[pltpu.VMEM((rows, 1), F32), pltpu.VMEM((rows, 1), F32),
                        pltpu.VMEM((rows, HEAD_DIM), F32), pltpu.VMEM((rows, SEL_KEY_TILE), F32)],
        compiler_params=_params("parallel", "parallel", "arbitrary"),
        name="nsa_attention",
    )(p16, p16, p16, p16, p16, cmp_kv, cmp_kv, gates, p32, ov, onehot)


def _hgrn_body(hq_ref, hf_ref, hi_ref, hz_ref, lb_ref, nw_ref, o_ref, st_ref, *, layer):
    c = HG_CHUNK

    @pl.when(pl.program_id(2) == 0)
    def _():
        st_ref[...] = jnp.zeros_like(st_ref)

    lbraw = lb_ref[...]
    ex = jnp.exp(lbraw - jnp.max(lbraw, axis=0, keepdims=True))
    sm = ex / jnp.sum(ex, axis=0, keepdims=True)
    lb = jnp.zeros((1, HG_DK), F32)
    for r in range(1, layer + 1):
        lb = lb + sm[r:r + 1]

    hf = hf_ref[...]
    q = jax.nn.silu(hq_ref[...])
    v = hi_ref[...]
    vb = v.astype(BF16)
    f = lb + (1.0 - lb) * jax.nn.sigmoid(hf)
    k = (1.0 - lb) * jax.nn.sigmoid(-hf)
    lf = jnp.log(f)

    ri = lax.broadcasted_iota(jnp.int32, (c, 1), 0)
    ci = lax.broadcasted_iota(jnp.int32, (1, c), 1)
    tri = jnp.where(ci <= ri, 1.0, 0.0).astype(BF16)
    lf1 = lf.astype(BF16)
    rem = lf - lf1.astype(F32)
    lf2 = rem.astype(BF16)
    lf3 = (rem - lf2.astype(F32)).astype(BF16)
    g = _dot(tri, lf1) + _dot(tri, lf2) + _dot(tri, lf3)

    st = st_ref[...]
    o_inter = _dot_nt((q * jnp.exp(g)).astype(BF16), st.astype(BF16))

    nsub = c // HG_SUB
    o_blk = [o_inter[b * HG_SUB:(b + 1) * HG_SUB] for b in range(nsub)]

    w = HG_SUB
    while w < c:
        gref = jnp.concatenate(
            [jnp.broadcast_to(g[base + w - 1:base + w], (2 * w, HG_DK)) for base in range(0, c, 2 * w)],
            axis=0)
        is_q = jnp.bitwise_and(ri, w) == w
        e = jnp.exp(jnp.where(is_q, g - gref, gref - g))
        qt = (q * e).astype(BF16)
        kt = (k * e).astype(BF16)
        for base in range(0, c, 2 * w):
            a = _dot_nt(qt[base + w:base + 2 * w], kt[base:base + w])
            upd = _dot(a.astype(BF16), vb[base:base + w])
            for s in range(w // HG_SUB):
                b = (base + w) // HG_SUB + s
                o_blk[b] = o_blk[b] + upd[s * HG_SUB:(s + 1) * HG_SUB]
        w *= 2

    rl = lax.broadcasted_iota(jnp.int32, (HG_SUB, 1), 0)
    for b in range(nsub):
        r0 = b * HG_SUB
        qb = q[r0:r0 + HG_SUB]
        gb = g[r0:r0 + HG_SUB]
        acc = o_blk[b]
        for s in range(HG_SUB):
            sa = r0 + s
            live = rl >= s
            dec = jnp.exp(jnp.where(live, gb - g[sa:sa + 1], 0.0))
            a_col = jnp.sum(qb * (k[sa:sa + 1] * dec), axis=-1, keepdims=True)
            acc = acc + jnp.where(live, a_col, 0.0) * v[sa:sa + 1]
        o_blk[b] = acc
    o = jnp.concatenate(o_blk, axis=0)

    o = o * lax.rsqrt(jnp.mean(o * o, axis=-1, keepdims=True) + LN_EPS)
    o_ref[...] = (o * nw_ref[...] * jax.nn.silu(hz_ref[...])).astype(o_ref.dtype)

    g_last = g[c - 1:c]
    kd = (k * jnp.exp(g_last - g)).astype(BF16)
    st_ref[...] = st * jnp.exp(g_last) + _dot_tn(vb, kd)


def _hgrn(p32, lb_raw, norm_w, layer):
    b, t, _ = p32.shape
    base = NSA_WIDTH // HG_DK

    def col_spec(section):
        return pl.BlockSpec((None, HG_CHUNK, HG_DK),
                            lambda bi, h, c: (bi, c, base + section * HG_HEADS + h))

    return pl.pallas_call(
        functools.partial(_hgrn_body, layer=layer),
        grid=(b, HG_HEADS, t // HG_CHUNK),
        in_specs=[col_spec(0), col_spec(1), col_spec(2), col_spec(3),
                  pl.BlockSpec((N_EVEN, HG_DK), lambda bi, h, c: (0, h)),
                  pl.BlockSpec((1, HG_DV), lambda bi, h, c: (0, h))],
        out_specs=pl.BlockSpec((None, HG_CHUNK, HG_DV), lambda bi, h, c: (bi, c, h)),
        out_shape=jax.ShapeDtypeStruct((b, t, HG_WIDTH), BF16),
        scratch_shapes=[pltpu.VMEM((HG_DV, HG_DK), F32)],
        compiler_params=_params("parallel", "parallel", "arbitrary"),
        name="hgrn2",
    )(p32, p32, p32, p32, lb_raw, norm_w)


def _glu_body(x_ref, wa_ref, wb_ref, wz_ref, u_ref, sz_ref):
    x = x_ref[...].astype(BF16)
    u_ref[...] = _dot(x, wa_ref[...]) * jax.nn.sigmoid(_dot(x, wb_ref[...]))
    sz_ref[...] = jax.nn.silu(_dot(x, wz_ref[...]))


def _glu_proj(x, w, bm, bn):
    m, k = x.shape
    nb = CONV_CH // bn

    def w_spec(part):
        return pl.BlockSpec((k, bn), lambda i, j: (0, part * nb + j))

    out = jax.ShapeDtypeStruct((m, CONV_CH), F32)
    return pl.pallas_call(
        _glu_body,
        grid=(m // bm, nb),
        in_specs=[pl.BlockSpec((bm, k), lambda i, j: (i, 0)), w_spec(0), w_spec(1), w_spec(2)],
        out_specs=[pl.BlockSpec((bm, bn), lambda i, j: (i, j))] * 2,
        out_shape=[out, out],
        compiler_params=_params("parallel", "arbitrary"),
        name="conv_glu_proj",
    )(x, w, w, w)


def _conv_body(uc_ref, up_ref, sz_ref, w_ref, cb_ref, g_ref, b_ref, o_ref, buf_ref, c_ref):
    first_tile = pl.program_id(1) == 0
    buf_ref[0:CONV_HALO, :] = jnp.where(first_tile, 0.0, up_ref[...])
    buf_ref[CONV_HALO:CONV_HALO + CONV_ROWS, :] = uc_ref[...]
    lead = CONV_HALO - (CONV_K - 1)
    for cs in range(0, CONV_CH, CONV_STRIP):
        acc = jnp.broadcast_to(cb_ref[:, cs:cs + CONV_STRIP], (CONV_ROWS, CONV_STRIP))
        for k in range(CONV_K):
            acc = acc + buf_ref[lead + k:lead + k + CONV_ROWS, cs:cs + CONV_STRIP] * w_ref[k:k + 1, cs:cs + CONV_STRIP]
        c_ref[:, cs:cs + CONV_STRIP] = acc
    c = c_ref[...]
    mu = jnp.mean(c, axis=-1, keepdims=True)
    d = c - mu
    var = jnp.mean(d * d, axis=-1, keepdims=True)
    cn = d * lax.rsqrt(var + LN_EPS) * g_ref[...] + b_ref[...]
    o_ref[...] = (jax.nn.silu(cn) * sz_ref[...]).astype(o_ref.dtype)


def _conv_module(u, sz, conv_w, conv_b, ln_g, ln_b):
    b, t, ch = u.shape
    halo_per_tile = CONV_ROWS // CONV_HALO
    row = lambda: pl.BlockSpec((1, ch), lambda bi, i: (0, 0))
    tile = lambda: pl.BlockSpec((None, CONV_ROWS, ch), lambda bi, i: (bi, i, 0))
    return pl.pallas_call(
        _conv_body,
        grid=(b, t // CONV_ROWS),
        in_specs=[tile(),
                  pl.BlockSpec((None, CONV_HALO, ch),
                               lambda bi, i: (bi, jnp.maximum(i * halo_per_tile - 1, 0), 0)),
                  tile(),
                  pl.BlockSpec((CONV_HALO, ch), lambda bi, i: (0, 0)),
                  row(), row(), row()],
        out_specs=tile(),
        out_shape=jax.ShapeDtypeStruct((b, t, ch), BF16),
        scratch_shapes=[pltpu.VMEM((CONV_HALO + CONV_ROWS, ch), F32), pltpu.VMEM((CONV_ROWS, ch), F32)],
        compiler_params=_params("parallel", "arbitrary"),
        name="conv_module",
    )(u, u, sz, conv_w, conv_b, ln_g, ln_b)


def _post_body(*refs, nparts):
    y_refs = refs[:nparts]
    w_refs = refs[nparts:2 * nparts]
    x_ref, p_ref, plw_ref, gw_ref, lg_ref, lb_ref, o32_ref, o16_ref = refs[2 * nparts:]
    y = _dot(y_refs[0][...], w_refs[0][...])
    for r in range(1, nparts):
        y = y + _dot(y_refs[r][...], w_refs[r][...])
    h = DEEPNORM_ALPHA * x_ref[...] + y
    mu = jnp.mean(h, axis=-1, keepdims=True)
    d = h - mu
    var = jnp.mean(d * d, axis=-1, keepdims=True)
    x1 = d * lax.rsqrt(var + LN_EPS) * lg_ref[...] + lb_ref[...]
    gate = jax.nn.sigmoid(_dot(x1.astype(BF16), gw_ref[...]))
    ple = _dot(p_ref[...].astype(BF16), plw_ref[...])
    out = x1 + ple * gate
    o32_ref[...] = out
    o16_ref[...] = out.astype(BF16)


def _post(y_parts, w_parts, x, p, ple_w, gate_w, ln_g, ln_b, bm):
    m, d = x.shape
    nparts = len(y_parts)
    resident = lambda shape: pl.BlockSpec(shape, lambda i: (0, 0), pipeline_mode=pl.Buffered(1))
    in_specs = ([pl.BlockSpec((bm, yp.shape[1]), lambda i: (i, 0)) for yp in y_parts]
                + [resident(wp.shape) for wp in w_parts]
                + [pl.BlockSpec((bm, d), lambda i: (i, 0)),
                   pl.BlockSpec((bm, PLE_DIM), lambda i: (i, 0)),
                   resident(ple_w.shape), resident(gate_w.shape),
                   resident((1, d)), resident((1, d))])
    return pl.pallas_call(
        functools.partial(_post_body, nparts=nparts),
        grid=(m // bm,),
        in_specs=in_specs,
        out_specs=[pl.BlockSpec((bm, d), lambda i: (i, 0))] * 2,
        out_shape=[jax.ShapeDtypeStruct((m, d), F32), jax.ShapeDtypeStruct((m, d), BF16)],
        compiler_params=_params("parallel"),
        name="outproj_deepnorm_ple",
    )(*y_parts, *w_parts, x, p, ple_w, gate_w, ln_g, ln_b)


def _even_weights(w_in):
    offs = np.cumsum([0, NSA_WIDTH, KV_W, KV_W, KV_W, KV_W, KV_W, KV_W, NSA_HEADS * 3, NSA_WIDTH,
                      HG_WIDTH, HG_WIDTH, HG_WIDTH, HG_WIDTH])
    sec = lambda a: w_in[:, offs[a]:offs[a + 1]]
    w16 = jnp.concatenate([sec(0) * HEAD_DIM ** -0.5, sec(3), sec(4), sec(5), sec(6)], axis=1)
    wc = jnp.concatenate([sec(1), sec(2)], axis=1)
    w32 = jnp.concatenate([sec(8), sec(9), sec(10), sec(11), sec(12)], axis=1)
    per_group = NSA_HPG * 3
    wg = sec(7)
    wg = jnp.concatenate(
        [jnp.pad(wg[:, g * per_group:(g + 1) * per_group], ((0, 0), (0, HEAD_DIM - per_group)))
         for g in range(NSA_KV_GROUPS)], axis=1)
    return w16.astype(BF16), wc.astype(BF16), w32.astype(BF16), wg.astype(BF16)


def _even_layer(x16, b, t, w_in, pe_k, w1_k, w2_k, pe_v, w1_v, w2_v, hg_norm, lb_raw, layer):
    w16, wc, w32, wg = _even_weights(w_in)
    p16 = _matmul(x16, w16, BF16, 1024, 1024, "even_proj_qkv")
    pc = _matmul(x16, wc, BF16, 1024, 2 * KV_W, "even_proj_cmp")
    p32 = _matmul(x16, w32, F32, 1024, 1024, "even_proj_gates")
    gates = _matmul(x16, wg, F32, 1024, NSA_KV_GROUPS * HEAD_DIM, "even_proj_nsa_gates")
    pe = jnp.stack([pe_k, pe_v]).reshape(2, 1, CMP_LEN * HEAD_DIM)
    w1 = jnp.stack([w1_k, w1_v]).astype(BF16)
    w2 = jnp.stack([w2_k, w2_v]).astype(BF16)
    cmp_kv = _compress(pc.reshape(b, t, 2 * KV_W), pe, w1, w2)
    p32 = p32.reshape(b, t, -1)
    ya = _nsa(p16.reshape(b, t, -1), cmp_kv, gates.reshape(b, t, -1), p32)
    yo = _hgrn(p32, lb_raw, hg_norm.reshape(1, HG_WIDTH), layer)
    return ya.reshape(b * t, NSA_WIDTH), yo.reshape(b * t, HG_WIDTH)


def _odd_layer(x16, b, t, w_in, conv_w, conv_b, ln_g, ln_b):
    u, sz = _glu_proj(x16, w_in.astype(BF16), 1024, 512)
    conv_w = jnp.pad(conv_w, ((0, CONV_HALO - CONV_K), (0, 0)))
    y = _conv_module(u.reshape(b, t, CONV_CH), sz.reshape(b, t, CONV_CH), conv_w,
                     conv_b.reshape(1, -1), ln_g.reshape(1, -1), ln_b.reshape(1, -1))
    return y.reshape(b * t, CONV_CH)


def kernel(x, p, ev_w_in, ev_cmp_pe_k, ev_cmp_w1_k, ev_cmp_w2_k, ev_cmp_pe_v, ev_cmp_w1_v,
           ev_cmp_w2_v, ev_hg_norm, hgrn_lb, ev_w_out, od_w_in, od_conv_w, od_conv_b, od_ln_g,
           od_ln_b, od_w_out, post_ln_g, post_ln_b, ple_w, ple_gate_w):
    b, t, d = x.shape
    x32 = x.reshape(b * t, d)
    x16 = x32.astype(BF16)
    for i in range(DEPTH):
        j = i // 2
        if i % 2 == 0:
            ya, yo = _even_layer(x16, b, t, ev_w_in[j], ev_cmp_pe_k[j], ev_cmp_w1_k[j], ev_cmp_w2_k[j],
                                 ev_cmp_pe_v[j], ev_cmp_w1_v[j], ev_cmp_w2_v[j], ev_hg_norm[j],
                                 hgrn_lb, j)
            w_out = ev_w_out[j].astype(BF16)
            y_parts = [ya, yo]
            w_parts = [w_out[:NSA_WIDTH], w_out[NSA_WIDTH:]]
        else:
            y_parts = [_odd_layer(x16, b, t, od_w_in[j], od_conv_w[j], od_conv_b[j], od_ln_g[j],
                                  od_ln_b[j])]
            w_parts = [od_w_out[j].astype(BF16)]
        x32, x16 = _post(y_parts, w_parts, x32, p[i].reshape(b * t, PLE_DIM), ple_w[i].astype(BF16),
                         ple_gate_w[i].astype(BF16), post_ln_g[i].reshape(1, d),
                         post_ln_b[i].reshape(1, d), 256)
    return x32.reshape(b, t, d)
```

```python
import functools

import numpy as np
import jax
import jax.numpy as jnp
from jax import lax
from jax.experimental import pallas as pl
from jax.experimental.pallas import tpu as pltpu

F32 = jnp.float32
BF16 = jnp.bfloat16

D_MODEL = 2048
DEPTH = 4
PLE_DIM = 256
HEAD_DIM = 128
NSA_HEADS = 8
NSA_KV_GROUPS = 2
NSA_HPG = NSA_HEADS // NSA_KV_GROUPS
NSA_WIDTH = NSA_HEADS * HEAD_DIM
KV_W = NSA_KV_GROUPS * HEAD_DIM
CMP_LEN = 32
CMP_STRIDE = 16
SEL_BLOCK = 64
SEL_TOPK = 16
WINDOW = 512
Q_BLOCK = 128
HG_HEADS = 8
HG_DK = 128
HG_DV = 128
HG_WIDTH = HG_HEADS * HG_DV
CONV_CH = D_MODEL
CONV_K = 31
N_EVEN = (DEPTH + 1) // 2
DEEPNORM_ALPHA = (2.0 * DEPTH) ** 0.25
LN_EPS = 1e-5

NEG = -1e30
SEL_KEY_TILE = 512
NSA_BATCH_PER_STEP = 1
HG_CHUNK = 128
HG_SUB = 4
HG_HEADS_PER_STEP = 4
CONV_ROWS = 128
CONV_HALO = 32
CONV_STRIP = 256
SUBLANES = 8
VMEM_LIMIT = 56 * 1024 * 1024


def _params(*sem):
    return pltpu.CompilerParams(dimension_semantics=sem, vmem_limit_bytes=VMEM_LIMIT)


def _dot(a, b):
    return jnp.dot(a, b, preferred_element_type=F32)


def _dot_nt(a, b):
    return lax.dot_general(a, b, (((1,), (1,)), ((), ())), preferred_element_type=F32)


def _dot_tn(a, b):
    return lax.dot_general(a, b, (((0,), (0,)), ((), ())), preferred_element_type=F32)


def _mm_body(x_ref, w_ref, o_ref):
    o_ref[...] = _dot(x_ref[...].astype(BF16), w_ref[...]).astype(o_ref.dtype)


def _matmul(x, w, out_dtype, bm, bn, name):
    m, k = x.shape
    n = w.shape[1]
    return pl.pallas_call(
        _mm_body,
        grid=(m // bm, n // bn),
        in_specs=[pl.BlockSpec((bm, k), lambda i, j: (i, 0)),
                  pl.BlockSpec((k, bn), lambda i, j: (0, j))],
        out_specs=pl.BlockSpec((bm, bn), lambda i, j: (i, j)),
        out_shape=jax.ShapeDtypeStruct((m, n), out_dtype),
        compiler_params=_params("parallel", "arbitrary"),
        name=name,
    )(x, w)


def _cmp_body(r_ref, pe_ref, w1_ref, w2_ref, o_ref, *, n):
    row = lax.broadcasted_iota(jnp.int32, (n, 1), 0)
    for kv in range(2):
        pe8 = jnp.broadcast_to(pe_ref[kv], (8, CMP_LEN * HEAD_DIM)).astype(BF16)
        pe_term = _dot(pe8, w1_ref[kv])[0:1, :]
        for g in range(NSA_KV_GROUPS):
            first = jnp.zeros((n, HEAD_DIM), F32)
            second = jnp.zeros((n, HEAD_DIM), F32)
            for r in range(CMP_STRIDE):
                col = r * 2 * KV_W + kv * KV_W + g * HEAD_DIM
                xr = r_ref[:, col:col + HEAD_DIM]
                first = first + _dot(xr, w1_ref[kv, r * HEAD_DIM:(r + 1) * HEAD_DIM, :])
                lo = (CMP_STRIDE + r) * HEAD_DIM
                second = second + _dot(xr, w1_ref[kv, lo:lo + HEAD_DIM, :])
            pre = first + pltpu.roll(second, n - 1, 0) + pe_term
            out = _dot(jax.nn.gelu(pre).astype(BF16), w2_ref[kv])
            o_ref[kv, g] = jnp.where(row < n - 1, out, 0.0).astype(o_ref.dtype)


def _compress(pc, pe, w1, w2):
    b, t, _ = pc.shape
    n = t // CMP_STRIDE
    r = pc.reshape(b, n, CMP_STRIDE * 2 * KV_W)
    return pl.pallas_call(
        functools.partial(_cmp_body, n=n),
        grid=(b,),
        in_specs=[pl.BlockSpec((None, n, CMP_STRIDE * 2 * KV_W), lambda i: (i, 0, 0)),
                  pl.BlockSpec((2, 1, CMP_LEN * HEAD_DIM), lambda i: (0, 0, 0)),
                  pl.BlockSpec((2, CMP_LEN * HEAD_DIM, HEAD_DIM), lambda i: (0, 0, 0)),
                  pl.BlockSpec((2, HEAD_DIM, HEAD_DIM), lambda i: (0, 0, 0))],
        out_specs=pl.BlockSpec((None, 2, NSA_KV_GROUPS, n, HEAD_DIM), lambda i: (i, 0, 0, 0, 0)),
        out_shape=jax.ShapeDtypeStruct((b, 2, NSA_KV_GROUPS, n, HEAD_DIM), BF16),
        compiler_params=_params("parallel"),
        name="nsa_compress",
    )(r, pe, w1, w2)


def _masked_softmax(s, mask):
    m = jnp.max(s, axis=-1, keepdims=True)
    e = jnp.where(mask, jnp.exp(s - m), 0.0)
    l = jnp.sum(e, axis=-1, keepdims=True)
    return e / jnp.where(l > 0.0, l, 1.0)


def _nsa_body(q_ref, ks_ref, vs_ref, kw_ref, vw_ref, kc_ref, vc_ref, gate_ref, nz_ref, ov_ref,
              eh_ref, o_ref, m_ref, l_ref, acc_ref, ar_ref, *, t_len, nb):
    i = pl.program_id(1)
    t0 = i * Q_BLOCK
    rows = NSA_HPG * Q_BLOCK
    n_cmp = t_len // CMP_STRIDE
    n_sel = t_len // SEL_BLOCK
    kt = SEL_KEY_TILE
    streams = [(bb, g) for bb in range(nb) for g in range(NSA_KV_GROUPS)]

    def lanes(g, width):
        return slice(g * width, (g + 1) * width)

    rowi = lax.broadcasted_iota(jnp.int32, (rows, 1), 0)
    tcol = t0 + jnp.bitwise_and(rowi, Q_BLOCK - 1)
    head_slope = lambda g, h: 2.0 ** -(g * NSA_HPG + h + 1)
    slopes = [jnp.concatenate([jnp.full((Q_BLOCK, 1), head_slope(g, h), F32) for h in range(NSA_HPG)],
                              axis=0) for g in range(NSA_KV_GROUPS)]
    q4 = [jnp.concatenate([q_ref[bb, :, g * rows + h * HEAD_DIM:g * rows + (h + 1) * HEAD_DIM]
                           for h in range(NSA_HPG)], axis=0) for bb, g in streams]

    nidx = lax.broadcasted_iota(jnp.int32, (1, n_cmp), 1)
    d_c = tcol - (nidx * CMP_STRIDE + CMP_LEN - 1)
    mask_c = d_c >= 0
    d_cf = d_c.astype(F32)
    o_c, imp = [], []
    for si, (bb, g) in enumerate(streams):
        s = jnp.where(mask_c, _dot_nt(q4[si], kc_ref[bb, g]) - slopes[g] * d_cf, NEG)
        p_c = _masked_softmax(s, mask_c)
        o_c.append(_dot(p_c.astype(BF16), vc_ref[bb, g]))
        p_sum = p_c[0:Q_BLOCK]
        for h in range(1, NSA_HPG):
            p_sum = p_sum + p_c[h * Q_BLOCK:(h + 1) * Q_BLOCK]
        p_hi = p_sum.astype(BF16)
        p_lo = (p_sum - p_hi.astype(F32)).astype(BF16)
        imp.append(_dot(p_hi, ov_ref[...]) + _dot(p_lo, ov_ref[...]))

    jidx = lax.broadcasted_iota(jnp.int32, (1, n_sel), 1)
    jf = jidx.astype(F32)
    tt = t0 + lax.broadcasted_iota(jnp.int32, (Q_BLOCK, 1), 0)
    cur = tt // SEL_BLOCK
    forced = (jidx == 0) | (jidx == cur) | (jidx == cur - 1)
    future = jidx * SEL_BLOCK > tt
    work = [jnp.where(future, -1e9, jnp.where(forced, 1e9, im)) for im in imp]
    picked = [jnp.zeros((Q_BLOCK, n_sel), F32) for _ in streams]
    for _ in range(min(SEL_TOPK, n_sel)):
        for si in range(len(streams)):
            mx = jnp.max(work[si], axis=-1, keepdims=True)
            first = jnp.min(jnp.where(work[si] == mx, jf, 1e9), axis=-1, keepdims=True)
            pick = jf == first
            picked[si] = jnp.where(pick, 1.0, picked[si])
            work[si] = jnp.where(pick, -3e38, work[si])
    sel = [jnp.where(future, 0.0, pk) > 0.5 for pk in picked]

    rel = ((t0 // SEL_BLOCK) - jidx).astype(F32) * float(SEL_BLOCK)
    in_block = jnp.bitwise_and(lax.broadcasted_iota(jnp.int32, (1, kt), 1), SEL_BLOCK - 1).astype(F32)
    for g in range(NSA_KV_GROUPS):
        ar_ref[g] = slopes[g] * in_block
    q_aug = []
    for si, (bb, g) in enumerate(streams):
        parts = []
        for h in range(NSA_HPG):
            bias_h = jnp.where(sel[si], -head_slope(g, h) * rel, NEG).astype(BF16)
            parts.append(jnp.concatenate([q4[si][h * Q_BLOCK:(h + 1) * Q_BLOCK], bias_h], axis=1))
        q_aug.append(jnp.concatenate(parts, axis=0))
        m_ref[si] = jnp.full((rows, 1), NEG, F32)
        l_ref[si] = jnp.zeros((rows, 1), F32)
        acc_ref[si] = jnp.zeros((rows, HEAD_DIM), F32)

    def sel_tile(c, causal):
        k0 = pl.multiple_of(c * kt, kt)
        eh = eh_ref[pl.ds(k0, kt), :]
        for si, (bb, g) in enumerate(streams):
            k_aug = jnp.concatenate([ks_ref[bb, pl.ds(k0, kt), lanes(g, HEAD_DIM)], eh], axis=1)
            sc = _dot_nt(q_aug[si], k_aug) + ar_ref[g]
            if causal:
                kpos = k0 + lax.broadcasted_iota(jnp.int32, (1, kt), 1)
                sc = jnp.where(kpos <= tcol, sc, NEG)
            m_old = m_ref[si]
            m_new = jnp.maximum(m_old, jnp.max(sc, axis=-1, keepdims=True))
            alpha = jnp.exp(m_old - m_new)
            p = jnp.exp(sc - m_new)
            l_ref[si] = alpha * l_ref[si] + jnp.sum(p, axis=-1, keepdims=True)
            acc_ref[si] = alpha * acc_ref[si] + _dot(p.astype(BF16),
                                                     vs_ref[bb, pl.ds(k0, kt), lanes(g, HEAD_DIM)])
            m_ref[si] = m_new

    c_last = t0 // kt

    def past_tile(c, carry):
        sel_tile(c, False)
        return carry

    lax.fori_loop(0, c_last, past_tile, 0)
    sel_tile(c_last, True)

    span = WINDOW + Q_BLOCK
    start = pl.multiple_of(jnp.maximum(t0 - WINDOW, 0), Q_BLOCK)
    kpos = start + lax.broadcasted_iota(jnp.int32, (1, span), 1)
    d_w = tcol - kpos
    mask_w = (d_w >= 0) & (d_w < WINDOW)
    d_wf = d_w.astype(F32)

    for si, (bb, g) in enumerate(streams):
        kw = kw_ref[bb, pl.ds(start, span), lanes(g, HEAD_DIM)]
        s = jnp.where(mask_w, _dot_nt(q4[si], kw) - slopes[g] * d_wf, NEG)
        p_w = _masked_softmax(s, mask_w)
        o_w = _dot(p_w.astype(BF16), vw_ref[bb, pl.ds(start, span), lanes(g, HEAD_DIM)])
        o_s = acc_ref[si] / l_ref[si]
        gt = jax.nn.sigmoid(gate_ref[bb, :, lanes(g, HEAD_DIM)])
        outs = []
        for h in range(NSA_HPG):
            r0, r1 = h * Q_BLOCK, (h + 1) * Q_BLOCK
            a = (gt[:, 3 * h:3 * h + 1] * o_c[si][r0:r1] + gt[:, 3 * h + 1:3 * h + 2] * o_s[r0:r1]
                 + gt[:, 3 * h + 2:3 * h + 3] * o_w[r0:r1])
            c0 = g * rows + h * HEAD_DIM
            outs.append(a * jax.nn.silu(nz_ref[bb, :, c0:c0 + HEAD_DIM]))
        o_ref[bb, :, lanes(g, rows)] = jnp.concatenate(outs, axis=1).astype(o_ref.dtype)


def _nsa_constants(t_len):
    n_cmp = t_len // CMP_STRIDE
    n_sel = t_len // SEL_BLOCK
    s = np.arange(n_cmp)[:, None] * CMP_STRIDE
    b = np.arange(n_sel)[None, :] * SEL_BLOCK
    ov = np.clip(np.minimum(s + CMP_LEN, b + SEL_BLOCK) - np.maximum(s, b), 0, None) / CMP_LEN
    ov[n_cmp - 1] = 0.0
    onehot = (np.arange(t_len)[:, None] // SEL_BLOCK == np.arange(n_sel)[None, :])
    return jnp.asarray(ov, BF16), jnp.asarray(onehot, BF16)


def _nsa(p16, cmp_kv, gates, p32):
    b, t, _ = p16.shape
    nb = NSA_BATCH_PER_STEP
    n_cmp = t // CMP_STRIDE
    n_sel = t // SEL_BLOCK
    rows = NSA_HPG * Q_BLOCK
    nstreams = nb * NSA_KV_GROUPS
    ov, onehot = _nsa_constants(t)
    kv_base = NSA_WIDTH // KV_W
    once = pl.Buffered(1)

    def kv_spec(section):
        return pl.BlockSpec((nb, t, KV_W), lambda bi, i: (bi, 0, kv_base + section), pipeline_mode=once)

    def cmp_spec(kv):
        return pl.BlockSpec((nb, None, NSA_KV_GROUPS, n_cmp, HEAD_DIM), lambda bi, i: (bi, kv, 0, 0, 0),
                            pipeline_mode=once)

    return pl.pallas_call(
        functools.partial(_nsa_body, t_len=t, nb=nb),
        grid=(b // nb, t // Q_BLOCK),
        in_specs=[
            pl.BlockSpec((nb, Q_BLOCK, NSA_WIDTH), lambda bi, i: (bi, i, 0)),
            kv_spec(0), kv_spec(1), kv_spec(2), kv_spec(3),
            cmp_spec(0), cmp_spec(1),
            pl.BlockSpec((nb, Q_BLOCK, NSA_KV_GROUPS * HEAD_DIM), lambda bi, i: (bi, i, 0)),
            pl.BlockSpec((nb, Q_BLOCK, NSA_WIDTH), lambda bi, i: (bi, i, 0)),
            pl.BlockSpec((n_cmp, n_sel), lambda bi, i: (0, 0), pipeline_mode=once),
            pl.BlockSpec((t, n_sel), lambda bi, i: (0, 0), pipeline_mode=once),
        ],
        out_specs=pl.BlockSpec((nb, Q_BLOCK, NSA_WIDTH), lambda bi, i: (bi, i, 0)),
        out_shape=jax.ShapeDtypeStruct((b, t, NSA_WIDTH), BF16),
        scratch_shapes=[pltpu.VMEM((nstreams, rows, 1), F32), pltpu.VMEM((nstreams, rows, 1), F32),
                        pltpu.VMEM((nstreams, rows, HEAD_DIM), F32),
                        pltpu.VMEM((NSA_KV_GROUPS, rows, SEL_KEY_TILE), F32)],
        compiler_params=_params("parallel", "arbitrary"),
        name="nsa_attention",
    )(p16, p16, p16, p16, p16, cmp_kv, cmp_kv, gates, p32, ov, onehot)


def _hgrn_level_masks():
    t = np.arange(HG_CHUNK)[:, None]
    s = np.arange(HG_CHUNK)[None, :]
    masks = []
    w = HG_SUB
    while w < HG_CHUNK:
        masks.append((t // (2 * w) == s // (2 * w)) & (t % (2 * w) >= w) & (s % (2 * w) < w))
        w *= 2
    return jnp.asarray(np.stack(masks), F32)


def _hgrn_body(hq_ref, hf_ref, hi_ref, hz_ref, lb_ref, nw_ref, mk_ref, o_ref, st_ref, *, layer):
    c = HG_CHUNK

    @pl.when(pl.program_id(2) == 0)
    def _():
        st_ref[...] = jnp.zeros_like(st_ref)

    ri = lax.broadcasted_iota(jnp.int32, (c, 1), 0)
    ci = lax.broadcasted_iota(jnp.int32, (1, c), 1)
    tri = jnp.where(ci <= ri, 1.0, 0.0).astype(BF16)
    in_sub = jnp.bitwise_and(ri, HG_SUB - 1)

    for hh in range(HG_HEADS_PER_STEP):
        ln = slice(hh * HG_DK, (hh + 1) * HG_DK)
        lbraw = lb_ref[:, ln]
        ex = jnp.exp(lbraw - jnp.max(lbraw, axis=0, keepdims=True))
        sm = ex / jnp.sum(ex, axis=0, keepdims=True)
        lb = jnp.zeros((1, HG_DK), F32)
        for r in range(1, layer + 1):
            lb = lb + sm[r:r + 1]

        hf = hf_ref[:, ln]
        q = jax.nn.silu(hq_ref[:, ln])
        v = hi_ref[:, ln]
        vb = v.astype(BF16)
        f = lb + (1.0 - lb) * jax.nn.sigmoid(hf)
        k = (1.0 - lb) * jax.nn.sigmoid(-hf)
        lf = jnp.log(f)

        lf1 = lf.astype(BF16)
        rem = lf - lf1.astype(F32)
        lf2 = rem.astype(BF16)
        lf3 = (rem - lf2.astype(F32)).astype(BF16)
        g = _dot(tri, lf1) + _dot(tri, lf2) + _dot(tri, lf3)

        st = st_ref[hh]
        o = _dot_nt((q * jnp.exp(g)).astype(BF16), st.astype(BF16))

        a_off = jnp.zeros((c, c), F32)
        w = HG_SUB
        level = 0
        while w < c:
            gref = jnp.concatenate(
                [jnp.broadcast_to(g[base + w - 1:base + w], (2 * w, HG_DK)) for base in range(0, c, 2 * w)],
                axis=0)
            is_q = jnp.bitwise_and(ri, w) == w
            e = jnp.exp(jnp.where(is_q, g - gref, gref - g))
            a_off = a_off + mk_ref[level] * _dot_nt((q * e).astype(BF16), (k * e).astype(BF16))
            w *= 2
            level += 1
        o = o + _dot(a_off.astype(BF16), vb)

        o = o + jnp.sum(q * k, axis=-1, keepdims=True) * v
        for d in range(1, HG_SUB):
            live = in_sub >= d
            dec = jnp.exp(jnp.where(live, g - pltpu.roll(g, d, 0), 0.0))
            a_col = jnp.sum(q * (pltpu.roll(k, d, 0) * dec), axis=-1, keepdims=True)
            o = o + jnp.where(live, a_col, 0.0) * pltpu.roll(v, d, 0)

        o = o * lax.rsqrt(jnp.mean(o * o, axis=-1, keepdims=True) + LN_EPS)
        o_ref[:, ln] = (o * nw_ref[:, ln] * jax.nn.silu(hz_ref[:, ln])).astype(o_ref.dtype)

        g_last = g[c - 1:c]
        kd = (k * jnp.exp(g_last - g)).astype(BF16)
        st_ref[hh] = st * jnp.exp(g_last) + _dot_tn(vb, kd)


def _hgrn(p32, lb_raw, norm_w, layer):
    b, t, _ = p32.shape
    hps = HG_HEADS_PER_STEP
    wide = hps * HG_DK
    base = NSA_WIDTH // wide
    per_section = HG_WIDTH // wide
    masks = _hgrn_level_masks()

    def col_spec(section):
        return pl.BlockSpec((None, HG_CHUNK, wide), lambda bi, h, c: (bi, c, base + section * per_section + h))

    return pl.pallas_call(
        functools.partial(_hgrn_body, layer=layer),
        grid=(b, HG_HEADS // hps, t // HG_CHUNK),
        in_specs=[col_spec(0), col_spec(1), col_spec(2), col_spec(3),
                  pl.BlockSpec((N_EVEN, wide), lambda bi, h, c: (0, h)),
                  pl.BlockSpec((1, wide), lambda bi, h, c: (0, h)),
                  pl.BlockSpec(masks.shape, lambda bi, h, c: (0, 0, 0))],
        out_specs=pl.BlockSpec((None, HG_CHUNK, wide), lambda bi, h, c: (bi, c, h)),
        out_shape=jax.ShapeDtypeStruct((b, t, HG_WIDTH), BF16),
        scratch_shapes=[pltpu.VMEM((hps, HG_DV, HG_DK), F32)],
        compiler_params=_params("parallel", "parallel", "arbitrary"),
        name="hgrn2",
    )(p32, p32, p32, p32, lb_raw, norm_w, masks)


def _glu_body(x_ref, wa_ref, wb_ref, wz_ref, u_ref, sz_ref):
    x = x_ref[...].astype(BF16)
    u_ref[...] = _dot(x, wa_ref[...]) * jax.nn.sigmoid(_dot(x, wb_ref[...]))
    sz_ref[...] = jax.nn.silu(_dot(x, wz_ref[...]))


def _glu_proj(x, w, bm, bn):
    m, k = x.shape
    nb = CONV_CH // bn

    def w_spec(part):
        return pl.BlockSpec((k, bn), lambda i, j: (0, part * nb + j))

    out = jax.ShapeDtypeStruct((m, CONV_CH), F32)
    return pl.pallas_call(
        _glu_body,
        grid=(m // bm, nb),
        in_specs=[pl.BlockSpec((bm, k), lambda i, j: (i, 0)), w_spec(0), w_spec(1), w_spec(2)],
        out_specs=[pl.BlockSpec((bm, bn), lambda i, j: (i, j))] * 2,
        out_shape=[out, out],
        compiler_params=_params("parallel", "arbitrary"),
        name="conv_glu_proj",
    )(x, w, w, w)


def _conv_body(uc_ref, up_ref, sz_ref, w_ref, cb_ref, g_ref, b_ref, o_ref, buf_ref, c_ref):
    first_tile = pl.program_id(1) == 0
    buf_ref[0:CONV_HALO, :] = jnp.where(first_tile, 0.0, up_ref[...])
    buf_ref[CONV_HALO:CONV_HALO + CONV_ROWS, :] = uc_ref[...]
    lead = CONV_HALO - (CONV_K - 1)
    buf_rows = CONV_HALO + CONV_ROWS
    for cs in range(0, CONV_CH, CONV_STRIP):
        x = buf_ref[:, cs:cs + CONV_STRIP]
        acc = jnp.broadcast_to(cb_ref[:, cs:cs + CONV_STRIP], (CONV_ROWS, CONV_STRIP))
        for phase in range(SUBLANES):
            xr = x if phase == 0 else pltpu.roll(x, buf_rows - phase, 0)
            for k in range(CONV_K):
                if (lead + k) % SUBLANES == phase:
                    a = lead + k - phase
                    acc = acc + xr[a:a + CONV_ROWS] * w_ref[k:k + 1, cs:cs + CONV_STRIP]
        c_ref[:, cs:cs + CONV_STRIP] = acc
    c = c_ref[...]
    mu = jnp.mean(c, axis=-1, keepdims=True)
    d = c - mu
    var = jnp.mean(d * d, axis=-1, keepdims=True)
    cn = d * lax.rsqrt(var + LN_EPS) * g_ref[...] + b_ref[...]
    o_ref[...] = (jax.nn.silu(cn) * sz_ref[...]).astype(o_ref.dtype)


def _conv_module(u, sz, conv_w, conv_b, ln_g, ln_b):
    b, t, ch = u.shape
    halo_per_tile = CONV_ROWS // CONV_HALO
    row = lambda: pl.BlockSpec((1, ch), lambda bi, i: (0, 0))
    tile = lambda: pl.BlockSpec((None, CONV_ROWS, ch), lambda bi, i: (bi, i, 0))
    return pl.pallas_call(
        _conv_body,
        grid=(b, t // CONV_ROWS),
        in_specs=[tile(),
                  pl.BlockSpec((None, CONV_HALO, ch),
                               lambda bi, i: (bi, jnp.maximum(i * halo_per_tile - 1, 0), 0)),
                  tile(),
                  pl.BlockSpec((CONV_HALO, ch), lambda bi, i: (0, 0)),
                  row(), row(), row()],
        out_specs=tile(),
        out_shape=jax.ShapeDtypeStruct((b, t, ch), BF16),
        scratch_shapes=[pltpu.VMEM((CONV_HALO + CONV_ROWS, ch), F32), pltpu.VMEM((CONV_ROWS, ch), F32)],
        compiler_params=_params("parallel", "arbitrary"),
        name="conv_module",
    )(u, u, sz, conv_w, conv_b, ln_g, ln_b)


def _post_body(*refs, nparts):
    y_refs = refs[:nparts]
    w_refs = refs[nparts:2 * nparts]
    x_ref, p_ref, plw_ref, gw_ref, lg_ref, lb_ref, o32_ref, o16_ref = refs[2 * nparts:]
    y = _dot(y_refs[0][...], w_refs[0][...])
    for r in range(1, nparts):
        y = y + _dot(y_refs[r][...], w_refs[r][...])
    h = DEEPNORM_ALPHA * x_ref[...] + y
    mu = jnp.mean(h, axis=-1, keepdims=True)
    d = h - mu
    var = jnp.mean(d * d, axis=-1, keepdims=True)
    x1 = d * lax.rsqrt(var + LN_EPS) * lg_ref[...] + lb_ref[...]
    gate = jax.nn.sigmoid(_dot(x1.astype(BF16), gw_ref[...]))
    ple = _dot(p_ref[...].astype(BF16), plw_ref[...])
    out = x1 + ple * gate
    o32_ref[...] = out
    o16_ref[...] = out.astype(BF16)


def _post(y_parts, w_parts, x, p, ple_w, gate_w, ln_g, ln_b, bm):
    m, d = x.shape
    nparts = len(y_parts)
    resident = lambda shape: pl.BlockSpec(shape, lambda i: (0, 0), pipeline_mode=pl.Buffered(1))
    in_specs = ([pl.BlockSpec((bm, yp.shape[1]), lambda i: (i, 0)) for yp in y_parts]
                + [resident(wp.shape) for wp in w_parts]
                + [pl.BlockSpec((bm, d), lambda i: (i, 0)),
                   pl.BlockSpec((bm, PLE_DIM), lambda i: (i, 0)),
                   resident(ple_w.shape), resident(gate_w.shape),
                   resident((1, d)), resident((1, d))])
    return pl.pallas_call(
        functools.partial(_post_body, nparts=nparts),
        grid=(m // bm,),
        in_specs=in_specs,
        out_specs=[pl.BlockSpec((bm, d), lambda i: (i, 0))] * 2,
        out_shape=[jax.ShapeDtypeStruct((m, d), F32), jax.ShapeDtypeStruct((m, d), BF16)],
        compiler_params=_params("parallel"),
        name="outproj_deepnorm_ple",
    )(*y_parts, *w_parts, x, p, ple_w, gate_w, ln_g, ln_b)


def _even_weights(w_in):
    offs = np.cumsum([0, NSA_WIDTH, KV_W, KV_W, KV_W, KV_W, KV_W, KV_W, NSA_HEADS * 3, NSA_WIDTH,
                      HG_WIDTH, HG_WIDTH, HG_WIDTH, HG_WIDTH])
    sec = lambda a: w_in[:, offs[a]:offs[a + 1]]
    w16 = jnp.concatenate([sec(0) * HEAD_DIM ** -0.5, sec(3), sec(4), sec(5), sec(6)], axis=1)
    wc = jnp.concatenate([sec(1), sec(2)], axis=1)
    w32 = jnp.concatenate([sec(8), sec(9), sec(10), sec(11), sec(12)], axis=1)
    per_group = NSA_HPG * 3
    wg = sec(7)
    wg = jnp.concatenate(
        [jnp.pad(wg[:, g * per_group:(g + 1) * per_group], ((0, 0), (0, HEAD_DIM - per_group)))
         for g in range(NSA_KV_GROUPS)], axis=1)
    return w16.astype(BF16), wc.astype(BF16), w32.astype(BF16), wg.astype(BF16)


def _even_layer(x16, b, t, w_in, pe_k, w1_k, w2_k, pe_v, w1_v, w2_v, hg_norm, lb_raw, layer):
    w16, wc, w32, wg = _even_weights(w_in)
    p16 = _matmul(x16, w16, BF16, 1024, 1024, "even_proj_qkv")
    pc = _matmul(x16, wc, BF16, 1024, 2 * KV_W, "even_proj_cmp")
    p32 = _matmul(x16, w32, F32, 1024, 1024, "even_proj_gates")
    gates = _matmul(x16, wg, F32, 1024, NSA_KV_GROUPS * HEAD_DIM, "even_proj_nsa_gates")
    pe = jnp.stack([pe_k, pe_v]).reshape(2, 1, CMP_LEN * HEAD_DIM)
    w1 = jnp.stack([w1_k, w1_v]).astype(BF16)
    w2 = jnp.stack([w2_k, w2_v]).astype(BF16)
    cmp_kv = _compress(pc.reshape(b, t, 2 * KV_W), pe, w1, w2)
    p32 = p32.reshape(b, t, -1)
    ya = _nsa(p16.reshape(b, t, -1), cmp_kv, gates.reshape(b, t, -1), p32)
    yo = _hgrn(p32, lb_raw, hg_norm.reshape(1, HG_WIDTH), layer)
    return ya.reshape(b * t, NSA_WIDTH), yo.reshape(b * t, HG_WIDTH)


def _odd_layer(x16, b, t, w_in, conv_w, conv_b, ln_g, ln_b):
    u, sz = _glu_proj(x16, w_in.astype(BF16), 1024, 512)
    conv_w = jnp.pad(conv_w, ((0, CONV_HALO - CONV_K), (0, 0)))
    y = _conv_module(u.reshape(b, t, CONV_CH), sz.reshape(b, t, CONV_CH), conv_w,
                     conv_b.reshape(1, -1), ln_g.reshape(1, -1), ln_b.reshape(1, -1))
    return y.reshape(b * t, CONV_CH)


def kernel(x, p, ev_w_in, ev_cmp_pe_k, ev_cmp_w1_k, ev_cmp_w2_k, ev_cmp_pe_v, ev_cmp_w1_v,
           ev_cmp_w2_v, ev_hg_norm, hgrn_lb, ev_w_out, od_w_in, od_conv_w, od_conv_b, od_ln_g,
           od_ln_b, od_w_out, post_ln_g, post_ln_b, ple_w, ple_gate_w):
    b, t, d = x.shape
    x32 = x.reshape(b * t, d)
    x16 = x32.astype(BF16)
    for i in range(DEPTH):
        j = i // 2
        if i % 2 == 0:
            ya, yo = _even_layer(x16, b, t, ev_w_in[j], ev_cmp_pe_k[j], ev_cmp_w1_k[j], ev_cmp_w2_k[j],
                                 ev_cmp_pe_v[j], ev_cmp_w1_v[j], ev_cmp_w2_v[j], ev_hg_norm[j],
                                 hgrn_lb, j)
            w_out = ev_w_out[j].astype(BF16)
            y_parts = [ya, yo]
            w_parts = [w_out[:NSA_WIDTH], w_out[NSA_WIDTH:]]
        else:
            y_parts = [_odd_layer(x16, b, t, od_w_in[j], od_conv_w[j], od_conv_b[j], od_ln_g[j],
                                  od_ln_b[j])]
            w_parts = [od_w_out[j].astype(BF16)]
        x32, x16 = _post(y_parts, w_parts, x32, p[i].reshape(b * t, PLE_DIM), ple_w[i].astype(BF16),
                         ple_gate_w[i].astype(BF16), post_ln_g[i].reshape(1, d),
                         post_ln_b[i].reshape(1, d), 256)
    return x32.reshape(b, t, d)
```

```python
import functools

import numpy as np
import jax
import jax.numpy as jnp
from jax import lax
from jax.experimental import pallas as pl
from jax.experimental.pallas import tpu as pltpu

F32 = jnp.float32
BF16 = jnp.bfloat16

D_MODEL = 2048
DEPTH = 4
PLE_DIM = 256
HEAD_DIM = 128
NSA_HEADS = 8
NSA_KV_GROUPS = 2
NSA_HPG = NSA_HEADS // NSA_KV_GROUPS
NSA_WIDTH = NSA_HEADS * HEAD_DIM
KV_W = NSA_KV_GROUPS * HEAD_DIM
CMP_LEN = 32
CMP_STRIDE = 16
SEL_BLOCK = 64
SEL_TOPK = 16
WINDOW = 512
Q_BLOCK = 128
HG_HEADS = 8
HG_DK = 128
HG_DV = 128
HG_WIDTH = HG_HEADS * HG_DV
CONV_CH = D_MODEL
CONV_K = 31
N_EVEN = (DEPTH + 1) // 2
DEEPNORM_ALPHA = (2.0 * DEPTH) ** 0.25
LN_EPS = 1e-5

NEG = -1e30
SEL_KEY_TILE = 2048
NSA_BATCH_PER_STEP = 1
HG_CHUNK = 128
HG_SUB = 4
HG_HEADS_PER_STEP = 4
CONV_ROWS = 128
CONV_HALO = 32
CONV_STRIP = 256
SUBLANES = 8
VMEM_LIMIT = 56 * 1024 * 1024


def _params(*sem):
    return pltpu.CompilerParams(dimension_semantics=sem, vmem_limit_bytes=VMEM_LIMIT)


def _dot(a, b):
    return jnp.dot(a, b, preferred_element_type=F32)


def _dot_nt(a, b):
    return lax.dot_general(a, b, (((1,), (1,)), ((), ())), preferred_element_type=F32)


def _dot_tn(a, b):
    return lax.dot_general(a, b, (((0,), (0,)), ((), ())), preferred_element_type=F32)


def _mm_body(x_ref, w_ref, o_ref):
    o_ref[...] = _dot(x_ref[...].astype(BF16), w_ref[...]).astype(o_ref.dtype)


def _matmul(x, w, out_dtype, bm, bn, name):
    m, k = x.shape
    n = w.shape[1]
    return pl.pallas_call(
        _mm_body,
        grid=(m // bm, n // bn),
        in_specs=[pl.BlockSpec((bm, k), lambda i, j: (i, 0)),
                  pl.BlockSpec((k, bn), lambda i, j: (0, j))],
        out_specs=pl.BlockSpec((bm, bn), lambda i, j: (i, j)),
        out_shape=jax.ShapeDtypeStruct((m, n), out_dtype),
        compiler_params=_params("parallel", "arbitrary"),
        name=name,
    )(x, w)


def _cmp_body(r_ref, pe_ref, w1_ref, w2_ref, o_ref, *, n):
    row = lax.broadcasted_iota(jnp.int32, (n, 1), 0)
    for kv in range(2):
        pe8 = jnp.broadcast_to(pe_ref[kv], (8, CMP_LEN * HEAD_DIM)).astype(BF16)
        pe_term = _dot(pe8, w1_ref[kv])[0:1, :]
        for g in range(NSA_KV_GROUPS):
            first = jnp.zeros((n, HEAD_DIM), F32)
            second = jnp.zeros((n, HEAD_DIM), F32)
            for r in range(CMP_STRIDE):
                col = r * 2 * KV_W + kv * KV_W + g * HEAD_DIM
                xr = r_ref[:, col:col + HEAD_DIM]
                first = first + _dot(xr, w1_ref[kv, r * HEAD_DIM:(r + 1) * HEAD_DIM, :])
                lo = (CMP_STRIDE + r) * HEAD_DIM
                second = second + _dot(xr, w1_ref[kv, lo:lo + HEAD_DIM, :])
            pre = first + pltpu.roll(second, n - 1, 0) + pe_term
            out = _dot(jax.nn.gelu(pre).astype(BF16), w2_ref[kv])
            o_ref[kv, g] = jnp.where(row < n - 1, out, 0.0).astype(o_ref.dtype)


def _compress(pc, pe, w1, w2):
    b, t, _ = pc.shape
    n = t // CMP_STRIDE
    r = pc.reshape(b, n, CMP_STRIDE * 2 * KV_W)
    return pl.pallas_call(
        functools.partial(_cmp_body, n=n),
        grid=(b,),
        in_specs=[pl.BlockSpec((None, n, CMP_STRIDE * 2 * KV_W), lambda i: (i, 0, 0)),
                  pl.BlockSpec((2, 1, CMP_LEN * HEAD_DIM), lambda i: (0, 0, 0)),
                  pl.BlockSpec((2, CMP_LEN * HEAD_DIM, HEAD_DIM), lambda i: (0, 0, 0)),
                  pl.BlockSpec((2, HEAD_DIM, HEAD_DIM), lambda i: (0, 0, 0))],
        out_specs=pl.BlockSpec((None, 2, NSA_KV_GROUPS, n, HEAD_DIM), lambda i: (i, 0, 0, 0, 0)),
        out_shape=jax.ShapeDtypeStruct((b, 2, NSA_KV_GROUPS, n, HEAD_DIM), BF16),
        compiler_params=_params("parallel"),
        name="nsa_compress",
    )(r, pe, w1, w2)


def _masked_softmax(s, mask):
    m = jnp.max(s, axis=-1, keepdims=True)
    e = jnp.where(mask, jnp.exp(s - m), 0.0)
    l = jnp.sum(e, axis=-1, keepdims=True)
    return e / jnp.where(l > 0.0, l, 1.0)


def _nsa_body(q_ref, ks_ref, vs_ref, kw_ref, vw_ref, kc_ref, vc_ref, gate_ref, nz_ref, ov_ref,
              eh_ref, o_ref, m_ref, l_ref, acc_ref, *, t_len, nb):
    i = pl.program_id(1)
    t0 = i * Q_BLOCK
    rows = NSA_HPG * Q_BLOCK
    n_cmp = t_len // CMP_STRIDE
    n_sel = t_len // SEL_BLOCK
    kt = SEL_KEY_TILE
    streams = [(bb, g) for bb in range(nb) for g in range(NSA_KV_GROUPS)]

    def lanes(g, width):
        return slice(g * width, (g + 1) * width)

    rowi = lax.broadcasted_iota(jnp.int32, (rows, 1), 0)
    tcol = t0 + jnp.bitwise_and(rowi, Q_BLOCK - 1)
    head_slope = lambda g, h: 2.0 ** -(g * NSA_HPG + h + 1)
    slopes = [jnp.concatenate([jnp.full((Q_BLOCK, 1), head_slope(g, h), F32) for h in range(NSA_HPG)],
                              axis=0) for g in range(NSA_KV_GROUPS)]
    q4 = [jnp.concatenate([q_ref[bb, :, g * rows + h * HEAD_DIM:g * rows + (h + 1) * HEAD_DIM]
                           for h in range(NSA_HPG)], axis=0) for bb, g in streams]

    nidx = lax.broadcasted_iota(jnp.int32, (1, n_cmp), 1)
    d_c = tcol - (nidx * CMP_STRIDE + CMP_LEN - 1)
    mask_c = d_c >= 0
    d_cf = d_c.astype(F32)
    o_c, imp = [], []
    for si, (bb, g) in enumerate(streams):
        s = jnp.where(mask_c, _dot_nt(q4[si], kc_ref[bb, g]) - slopes[g] * d_cf, NEG)
        p_c = _masked_softmax(s, mask_c)
        o_c.append(_dot(p_c.astype(BF16), vc_ref[bb, g]))
        p_sum = p_c[0:Q_BLOCK]
        for h in range(1, NSA_HPG):
            p_sum = p_sum + p_c[h * Q_BLOCK:(h + 1) * Q_BLOCK]
        p_hi = p_sum.astype(BF16)
        p_lo = (p_sum - p_hi.astype(F32)).astype(BF16)
        imp.append(_dot(p_hi, ov_ref[...]) + _dot(p_lo, ov_ref[...]))

    span = WINDOW + Q_BLOCK
    start = pl.multiple_of(jnp.maximum(t0 - WINDOW, 0), Q_BLOCK)
    kpos_w = start + lax.broadcasted_iota(jnp.int32, (1, span), 1)
    d_w = tcol - kpos_w
    mask_w = (d_w >= 0) & (d_w < WINDOW)
    d_wf = d_w.astype(F32)
    o_w = []
    for si, (bb, g) in enumerate(streams):
        kw = kw_ref[bb, pl.ds(start, span), lanes(g, HEAD_DIM)]
        s = jnp.where(mask_w, _dot_nt(q4[si], kw) - slopes[g] * d_wf, NEG)
        p_w = _masked_softmax(s, mask_w)
        o_w.append(_dot(p_w.astype(BF16), vw_ref[bb, pl.ds(start, span), lanes(g, HEAD_DIM)]))

    jrow = lax.broadcasted_iota(jnp.int32, (n_sel, 1), 0)
    jf = jrow.astype(F32)
    tt = t0 + lax.broadcasted_iota(jnp.int32, (1, Q_BLOCK), 1)
    cur = tt // SEL_BLOCK
    forced = (jrow == 0) | (jrow == cur) | (jrow == cur - 1)
    future = jrow * SEL_BLOCK > tt
    work = [jnp.where(future, -1e9, jnp.where(forced, 1e9, im.T)) for im in imp]
    picked = [jnp.zeros((n_sel, Q_BLOCK), F32) for _ in streams]
    for _ in range(min(SEL_TOPK, n_sel)):
        for si in range(len(streams)):
            mx = jnp.max(work[si], axis=0, keepdims=True)
            first = jnp.min(jnp.where(work[si] == mx, jf, 1e9), axis=0, keepdims=True)
            pick = jf == first
            picked[si] = jnp.where(pick, 1.0, picked[si])
            work[si] = jnp.where(pick, -3e38, work[si])
    sel = [jnp.where(future, 0.0, pk).T > 0.5 for pk in picked]

    jidx = lax.broadcasted_iota(jnp.int32, (1, n_sel), 1)
    blk_off = jidx.astype(F32) * float(SEL_BLOCK)
    q_aug = []
    for si, (bb, g) in enumerate(streams):
        parts = []
        for h in range(NSA_HPG):
            bias_h = jnp.where(jidx == 0, head_slope(g, h),
                               jnp.where(sel[si], head_slope(g, h) * blk_off, NEG)).astype(BF16)
            parts.append(jnp.concatenate([q4[si][h * Q_BLOCK:(h + 1) * Q_BLOCK], bias_h], axis=1))
        q_aug.append(jnp.concatenate(parts, axis=0))
        m_ref[si] = jnp.full((rows, 1), NEG, F32)
        l_ref[si] = jnp.zeros((rows, 1), F32)
        acc_ref[si] = jnp.zeros((rows, HEAD_DIM), F32)

    def sel_tile(c, causal):
        k0 = pl.multiple_of(c * kt, kt)
        eh = eh_ref[pl.ds(k0, kt), :]
        for si, (bb, g) in enumerate(streams):
            k_aug = jnp.concatenate([ks_ref[bb, pl.ds(k0, kt), lanes(g, HEAD_DIM)], eh], axis=1)
            sc = _dot_nt(q_aug[si], k_aug)
            if causal:
                kpos = k0 + lax.broadcasted_iota(jnp.int32, (1, kt), 1)
                sc = jnp.where(kpos <= tcol, sc, NEG)
            m_old = m_ref[si]
            m_new = jnp.maximum(m_old, jnp.max(sc, axis=-1, keepdims=True))
            alpha = jnp.exp(m_old - m_new)
            p = jnp.exp(sc - m_new)
            l_ref[si] = alpha * l_ref[si] + jnp.sum(p, axis=-1, keepdims=True)
            acc_ref[si] = alpha * acc_ref[si] + _dot(p.astype(BF16),
                                                     vs_ref[bb, pl.ds(k0, kt), lanes(g, HEAD_DIM)])
            m_ref[si] = m_new

    c_last = t0 // kt

    def past_tile(c, carry):
        sel_tile(c, False)
        return carry

    lax.fori_loop(0, c_last, past_tile, 0)
    sel_tile(c_last, True)

    for si, (bb, g) in enumerate(streams):
        o_s = acc_ref[si] / l_ref[si]
        gt = jax.nn.sigmoid(gate_ref[bb, :, lanes(g, HEAD_DIM)])
        outs = []
        for h in range(NSA_HPG):
            r0, r1 = h * Q_BLOCK, (h + 1) * Q_BLOCK
            a = (gt[:, 3 * h:3 * h + 1] * o_c[si][r0:r1] + gt[:, 3 * h + 1:3 * h + 2] * o_s[r0:r1]
                 + gt[:, 3 * h + 2:3 * h + 3] * o_w[si][r0:r1])
            c0 = g * rows + h * HEAD_DIM
            outs.append(a * jax.nn.silu(nz_ref[bb, :, c0:c0 + HEAD_DIM]))
        o_ref[bb, :, lanes(g, rows)] = jnp.concatenate(outs, axis=1).astype(o_ref.dtype)


def _nsa_constants(t_len):
    n_cmp = t_len // CMP_STRIDE
    n_sel = t_len // SEL_BLOCK
    s = np.arange(n_cmp)[:, None] * CMP_STRIDE
    b = np.arange(n_sel)[None, :] * SEL_BLOCK
    ov = np.clip(np.minimum(s + CMP_LEN, b + SEL_BLOCK) - np.maximum(s, b), 0, None) / CMP_LEN
    ov[n_cmp - 1] = 0.0
    key_cols = (np.arange(t_len)[:, None] // SEL_BLOCK == np.arange(n_sel)[None, :]).astype(np.float32)
    key_cols[:, 0] = np.arange(t_len) % SEL_BLOCK
    return jnp.asarray(ov, BF16), jnp.asarray(key_cols, BF16)


def _nsa(p16, cmp_kv, gates, p32):
    b, t, _ = p16.shape
    nb = NSA_BATCH_PER_STEP
    n_cmp = t // CMP_STRIDE
    n_sel = t // SEL_BLOCK
    rows = NSA_HPG * Q_BLOCK
    nstreams = nb * NSA_KV_GROUPS
    ov, key_cols = _nsa_constants(t)
    kv_base = NSA_WIDTH // KV_W
    once = pl.Buffered(1)

    def kv_spec(section):
        return pl.BlockSpec((nb, t, KV_W), lambda bi, i: (bi, 0, kv_base + section), pipeline_mode=once)

    def cmp_spec(kv):
        return pl.BlockSpec((nb, None, NSA_KV_GROUPS, n_cmp, HEAD_DIM), lambda bi, i: (bi, kv, 0, 0, 0),
                            pipeline_mode=once)

    return pl.pallas_call(
        functools.partial(_nsa_body, t_len=t, nb=nb),
        grid=(b // nb, t // Q_BLOCK),
        in_specs=[
            pl.BlockSpec((nb, Q_BLOCK, NSA_WIDTH), lambda bi, i: (bi, i, 0)),
            kv_spec(0), kv_spec(1), kv_spec(2), kv_spec(3),
            cmp_spec(0), cmp_spec(1),
            pl.BlockSpec((nb, Q_BLOCK, NSA_KV_GROUPS * HEAD_DIM), lambda bi, i: (bi, i, 0)),
            pl.BlockSpec((nb, Q_BLOCK, NSA_WIDTH), lambda bi, i: (bi, i, 0)),
            pl.BlockSpec((n_cmp, n_sel), lambda bi, i: (0, 0), pipeline_mode=once),
            pl.BlockSpec((t, n_sel), lambda bi, i: (0, 0), pipeline_mode=once),
        ],
        out_specs=pl.BlockSpec((nb, Q_BLOCK, NSA_WIDTH), lambda bi, i: (bi, i, 0)),
        out_shape=jax.ShapeDtypeStruct((b, t, NSA_WIDTH), BF16),
        scratch_shapes=[pltpu.VMEM((nstreams, rows, 1), F32), pltpu.VMEM((nstreams, rows, 1), F32),
                        pltpu.VMEM((nstreams, rows, HEAD_DIM), F32)],
        compiler_params=_params("parallel", "arbitrary"),
        name="nsa_attention",
    )(p16, p16, p16, p16, p16, cmp_kv, cmp_kv, gates, p32, ov, key_cols)


def _hgrn_level_masks():
    t = np.arange(HG_CHUNK)[:, None]
    s = np.arange(HG_CHUNK)[None, :]
    masks = []
    w = HG_SUB
    while w < HG_CHUNK:
        masks.append((t // (2 * w) == s // (2 * w)) & (t % (2 * w) >= w) & (s % (2 * w) < w))
        w *= 2
    return jnp.asarray(np.stack(masks), F32)


def _hgrn_body(hq_ref, hf_ref, hi_ref, hz_ref, lb_ref, nw_ref, mk_ref, o_ref, st_ref, *, layer):
    c = HG_CHUNK

    @pl.when(pl.program_id(2) == 0)
    def _():
        st_ref[...] = jnp.zeros_like(st_ref)

    ri = lax.broadcasted_iota(jnp.int32, (c, 1), 0)
    ci = lax.broadcasted_iota(jnp.int32, (1, c), 1)
    tri = jnp.where(ci <= ri, 1.0, 0.0).astype(BF16)
    in_sub = jnp.bitwise_and(ri, HG_SUB - 1)

    for hh in range(HG_HEADS_PER_STEP):
        ln = slice(hh * HG_DK, (hh + 1) * HG_DK)
        lbraw = lb_ref[:, ln]
        ex = jnp.exp(lbraw - jnp.max(lbraw, axis=0, keepdims=True))
        sm = ex / jnp.sum(ex, axis=0, keepdims=True)
        lb = jnp.zeros((1, HG_DK), F32)
        for r in range(1, layer + 1):
            lb = lb + sm[r:r + 1]

        hf = hf_ref[:, ln]
        q = jax.nn.silu(hq_ref[:, ln])
        v = hi_ref[:, ln]
        vb = v.astype(BF16)
        f = lb + (1.0 - lb) * jax.nn.sigmoid(hf)
        k = (1.0 - lb) * jax.nn.sigmoid(-hf)
        lf = jnp.log(f)

        lf1 = lf.astype(BF16)
        rem = lf - lf1.astype(F32)
        lf2 = rem.astype(BF16)
        lf3 = (rem - lf2.astype(F32)).astype(BF16)
        g = _dot(tri, lf1) + _dot(tri, lf2) + _dot(tri, lf3)

        st = st_ref[hh]
        o = _dot_nt((q * jnp.exp(g)).astype(BF16), st.astype(BF16))

        a_off = jnp.zeros((c, c), F32)
        w = HG_SUB
        level = 0
        while w < c:
            gref = jnp.concatenate(
                [jnp.broadcast_to(g[base + w - 1:base + w], (2 * w, HG_DK)) for base in range(0, c, 2 * w)],
                axis=0)
            is_q = jnp.bitwise_and(ri, w) == w
            e = jnp.exp(jnp.where(is_q, g - gref, gref - g))
            a_off = a_off + mk_ref[level] * _dot_nt((q * e).astype(BF16), (k * e).astype(BF16))
            w *= 2
            level += 1
        o = o + _dot(a_off.astype(BF16), vb)

        o = o + jnp.sum(q * k, axis=-1, keepdims=True) * v
        for d in range(1, HG_SUB):
            live = in_sub >= d
            dec = jnp.exp(jnp.where(live, g - pltpu.roll(g, d, 0), 0.0))
            a_col = jnp.sum(q * (pltpu.roll(k, d, 0) * dec), axis=-1, keepdims=True)
            o = o + jnp.where(live, a_col, 0.0) * pltpu.roll(v, d, 0)

        o = o * lax.rsqrt(jnp.mean(o * o, axis=-1, keepdims=True) + LN_EPS)
        o_ref[:, ln] = (o * nw_ref[:, ln] * jax.nn.silu(hz_ref[:, ln])).astype(o_ref.dtype)

        g_last = g[c - 1:c]
        kd = (k * jnp.exp(g_last - g)).astype(BF16)
        st_ref[hh] = st * jnp.exp(g_last) + _dot_tn(vb, kd)


def _hgrn(p32, lb_raw, norm_w, layer):
    b, t, _ = p32.shape
    hps = HG_HEADS_PER_STEP
    wide = hps * HG_DK
    base = NSA_WIDTH // wide
    per_section = HG_WIDTH // wide
    masks = _hgrn_level_masks()

    def col_spec(section):
        return pl.BlockSpec((None, HG_CHUNK, wide), lambda bi, h, c: (bi, c, base + section * per_section + h))

    return pl.pallas_call(
        functools.partial(_hgrn_body, layer=layer),
        grid=(b, HG_HEADS // hps, t // HG_CHUNK),
        in_specs=[col_spec(0), col_spec(1), col_spec(2), col_spec(3),
                  pl.BlockSpec((N_EVEN, wide), lambda bi, h, c: (0, h)),
                  pl.BlockSpec((1, wide), lambda bi, h, c: (0, h)),
                  pl.BlockSpec(masks.shape, lambda bi, h, c: (0, 0, 0))],
        out_specs=pl.BlockSpec((None, HG_CHUNK, wide), lambda bi, h, c: (bi, c, h)),
        out_shape=jax.ShapeDtypeStruct((b, t, HG_WIDTH), BF16),
        scratch_shapes=[pltpu.VMEM((hps, HG_DV, HG_DK), F32)],
        compiler_params=_params("parallel", "parallel", "arbitrary"),
        name="hgrn2",
    )(p32, p32, p32, p32, lb_raw, norm_w, masks)


def _glu_body(x_ref, wa_ref, wb_ref, wz_ref, u_ref, sz_ref):
    x = x_ref[...].astype(BF16)
    u_ref[...] = _dot(x, wa_ref[...]) * jax.nn.sigmoid(_dot(x, wb_ref[...]))
    sz_ref[...] = jax.nn.silu(_dot(x, wz_ref[...]))


def _glu_proj(x, w, bm, bn):
    m, k = x.shape
    nb = CONV_CH // bn

    def w_spec(part):
        return pl.BlockSpec((k, bn), lambda i, j: (0, part * nb + j))

    out = jax.ShapeDtypeStruct((m, CONV_CH), F32)
    return pl.pallas_call(
        _glu_body,
        grid=(m // bm, nb),
        in_specs=[pl.BlockSpec((bm, k), lambda i, j: (i, 0)), w_spec(0), w_spec(1), w_spec(2)],
        out_specs=[pl.BlockSpec((bm, bn), lambda i, j: (i, j))] * 2,
        out_shape=[out, out],
        compiler_params=_params("parallel", "arbitrary"),
        name="conv_glu_proj",
    )(x, w, w, w)


def _conv_body(uc_ref, up_ref, sz_ref, w_ref, cb_ref, g_ref, b_ref, o_ref, buf_ref, c_ref):
    first_tile = pl.program_id(1) == 0
    buf_ref[0:CONV_HALO, :] = jnp.where(first_tile, 0.0, up_ref[...])
    buf_ref[CONV_HALO:CONV_HALO + CONV_ROWS, :] = uc_ref[...]
    lead = CONV_HALO - (CONV_K - 1)
    buf_rows = CONV_HALO + CONV_ROWS
    for cs in range(0, CONV_CH, CONV_STRIP):
        x = buf_ref[:, cs:cs + CONV_STRIP]
        acc = jnp.broadcast_to(cb_ref[:, cs:cs + CONV_STRIP], (CONV_ROWS, CONV_STRIP))
        for phase in range(SUBLANES):
            xr = x if phase == 0 else pltpu.roll(x, buf_rows - phase, 0)
            for k in range(CONV_K):
                if (lead + k) % SUBLANES == phase:
                    a = lead + k - phase
                    acc = acc + xr[a:a + CONV_ROWS] * w_ref[k:k + 1, cs:cs + CONV_STRIP]
        c_ref[:, cs:cs + CONV_STRIP] = acc
    c = c_ref[...]
    mu = jnp.mean(c, axis=-1, keepdims=True)
    d = c - mu
    var = jnp.mean(d * d, axis=-1, keepdims=True)
    cn = d * lax.rsqrt(var + LN_EPS) * g_ref[...] + b_ref[...]
    o_ref[...] = (jax.nn.silu(cn) * sz_ref[...]).astype(o_ref.dtype)


def _conv_module(u, sz, conv_w, conv_b, ln_g, ln_b):
    b, t, ch = u.shape
    halo_per_tile = CONV_ROWS // CONV_HALO
    row = lambda: pl.BlockSpec((1, ch), lambda bi, i: (0, 0))
    tile = lambda: pl.BlockSpec((None, CONV_ROWS, ch), lambda bi, i: (bi, i, 0))
    return pl.pallas_call(
        _conv_body,
        grid=(b, t // CONV_ROWS),
        in_specs=[tile(),
                  pl.BlockSpec((None, CONV_HALO, ch),
                               lambda bi, i: (bi, jnp.maximum(i * halo_per_tile - 1, 0), 0)),
                  tile(),
                  pl.BlockSpec((CONV_HALO, ch), lambda bi, i: (0, 0)),
                  row(), row(), row()],
        out_specs=tile(),
        out_shape=jax.ShapeDtypeStruct((b, t, ch), BF16),
        scratch_shapes=[pltpu.VMEM((CONV_HALO + CONV_ROWS, ch), F32), pltpu.VMEM((CONV_ROWS, ch), F32)],
        compiler_params=_params("parallel", "arbitrary"),
        name="conv_module",
    )(u, u, sz, conv_w, conv_b, ln_g, ln_b)


def _post_body(*refs, nparts):
    y_refs = refs[:nparts]
    w_refs = refs[nparts:2 * nparts]
    x_ref, p_ref, plw_ref, gw_ref, lg_ref, lb_ref, o32_ref, o16_ref = refs[2 * nparts:]
    y = _dot(y_refs[0][...], w_refs[0][...])
    for r in range(1, nparts):
        y = y + _dot(y_refs[r][...], w_refs[r][...])
    h = DEEPNORM_ALPHA * x_ref[...] + y
    mu = jnp.mean(h, axis=-1, keepdims=True)
    d = h - mu
    var = jnp.mean(d * d, axis=-1, keepdims=True)
    x1 = d * lax.rsqrt(var + LN_EPS) * lg_ref[...] + lb_ref[...]
    gate = jax.nn.sigmoid(_dot(x1.astype(BF16), gw_ref[...]))
    ple = _dot(p_ref[...].astype(BF16), plw_ref[...])
    out = x1 + ple * gate
    o32_ref[...] = out
    o16_ref[...] = out.astype(BF16)


def _post(y_parts, w_parts, x, p, layer, ple_w, gate_w, ln_g, ln_b, bm):
    m, d = x.shape
    nparts = len(y_parts)
    resident = lambda shape: pl.BlockSpec(shape, lambda i: (0, 0), pipeline_mode=pl.Buffered(1))
    in_specs = ([pl.BlockSpec((bm, yp.shape[1]), lambda i: (i, 0)) for yp in y_parts]
                + [resident(wp.shape) for wp in w_parts]
                + [pl.BlockSpec((bm, d), lambda i: (i, 0)),
                   pl.BlockSpec((None, bm, PLE_DIM), lambda i: (layer, i, 0)),
                   resident(ple_w.shape), resident(gate_w.shape),
                   resident((1, d)), resident((1, d))])
    return pl.pallas_call(
        functools.partial(_post_body, nparts=nparts),
        grid=(m // bm,),
        in_specs=in_specs,
        out_specs=[pl.BlockSpec((bm, d), lambda i: (i, 0))] * 2,
        out_shape=[jax.ShapeDtypeStruct((m, d), F32), jax.ShapeDtypeStruct((m, d), BF16)],
        compiler_params=_params("parallel"),
        name="outproj_deepnorm_ple",
    )(*y_parts, *w_parts, x, p, ple_w, gate_w, ln_g, ln_b)


def _even_weights(w_in):
    offs = np.cumsum([0, NSA_WIDTH, KV_W, KV_W, KV_W, KV_W, KV_W, KV_W, NSA_HEADS * 3, NSA_WIDTH,
                      HG_WIDTH, HG_WIDTH, HG_WIDTH, HG_WIDTH])
    sec = lambda a: w_in[:, offs[a]:offs[a + 1]]
    w16 = jnp.concatenate([sec(0) * HEAD_DIM ** -0.5, sec(3), sec(4), sec(5), sec(6)], axis=1)
    wc = jnp.concatenate([sec(1), sec(2)], axis=1)
    w32 = jnp.concatenate([sec(8), sec(9), sec(10), sec(11), sec(12)], axis=1)
    per_group = NSA_HPG * 3
    wg = sec(7)
    wg = jnp.concatenate(
        [jnp.pad(wg[:, g * per_group:(g + 1) * per_group], ((0, 0), (0, HEAD_DIM - per_group)))
         for g in range(NSA_KV_GROUPS)], axis=1)
    return w16.astype(BF16), wc.astype(BF16), w32.astype(BF16), wg.astype(BF16)


def _even_layer(x16, b, t, w_in, pe_k, w1_k, w2_k, pe_v, w1_v, w2_v, hg_norm, lb_raw, layer):
    w16, wc, w32, wg = _even_weights(w_in)
    p16 = _matmul(x16, w16, BF16, 1024, 1024, "even_proj_qkv")
    pc = _matmul(x16, wc, BF16, 1024, 2 * KV_W, "even_proj_cmp")
    p32 = _matmul(x16, w32, F32, 1024, 1024, "even_proj_gates")
    gates = _matmul(x16, wg, F32, 1024, NSA_KV_GROUPS * HEAD_DIM, "even_proj_nsa_gates")
    pe = jnp.stack([pe_k, pe_v]).reshape(2, 1, CMP_LEN * HEAD_DIM)
    w1 = jnp.stack([w1_k, w1_v]).astype(BF16)
    w2 = jnp.stack([w2_k, w2_v]).astype(BF16)
    cmp_kv = _compress(pc.reshape(b, t, 2 * KV_W), pe, w1, w2)
    p32 = p32.reshape(b, t, -1)
    ya = _nsa(p16.reshape(b, t, -1), cmp_kv, gates.reshape(b, t, -1), p32)
    yo = _hgrn(p32, lb_raw, hg_norm.reshape(1, HG_WIDTH), layer)
    return ya.reshape(b * t, NSA_WIDTH), yo.reshape(b * t, HG_WIDTH)


def _odd_layer(x16, b, t, w_in, conv_w, conv_b, ln_g, ln_b):
    u, sz = _glu_proj(x16, w_in.astype(BF16), 1024, 512)
    conv_w = jnp.pad(conv_w, ((0, CONV_HALO - CONV_K), (0, 0)))
    y = _conv_module(u.reshape(b, t, CONV_CH), sz.reshape(b, t, CONV_CH), conv_w,
                     conv_b.reshape(1, -1), ln_g.reshape(1, -1), ln_b.reshape(1, -1))
    return y.reshape(b * t, CONV_CH)


def kernel(x, p, ev_w_in, ev_cmp_pe_k, ev_cmp_w1_k, ev_cmp_w2_k, ev_cmp_pe_v, ev_cmp_w1_v,
           ev_cmp_w2_v, ev_hg_norm, hgrn_lb, ev_w_out, od_w_in, od_conv_w, od_conv_b, od_ln_g,
           od_ln_b, od_w_out, post_ln_g, post_ln_b, ple_w, ple_gate_w):
    b, t, d = x.shape
    x32 = x.reshape(b * t, d)
    x16 = x32.astype(BF16)
    for i in range(DEPTH):
        j = i // 2
        if i % 2 == 0:
            ya, yo = _even_layer(x16, b, t, ev_w_in[j], ev_cmp_pe_k[j], ev_cmp_w1_k[j], ev_cmp_w2_k[j],
                                 ev_cmp_pe_v[j], ev_cmp_w1_v[j], ev_cmp_w2_v[j], ev_hg_norm[j],
                                 hgrn_lb, j)
            w_out = ev_w_out[j].astype(BF16)
            y_parts = [ya, yo]
            w_parts = [w_out[:NSA_WIDTH], w_out[NSA_WIDTH:]]
        else:
            y_parts = [_odd_layer(x16, b, t, od_w_in[j], od_conv_w[j], od_conv_b[j], od_ln_g[j],
                                  od_ln_b[j])]
            w_parts = [od_w_out[j].astype(BF16)]
        x32, x16 = _post(y_parts, w_parts, x32, p.reshape(DEPTH, b * t, PLE_DIM), i, ple_w[i].astype(BF16),
                         ple_gate_w[i].astype(BF16), post_ln_g[i].reshape(1, d),
                         post_ln_b[i].reshape(1, d), 256)
    return x32.reshape(b, t, d)
```

```python
import functools

import numpy as np
import jax
import jax.numpy as jnp
from jax import lax
from jax.experimental import pallas as pl
from jax.experimental.pallas import tpu as pltpu

F32 = jnp.float32
BF16 = jnp.bfloat16

D_MODEL = 2048
DEPTH = 4
PLE_DIM = 256
HEAD_DIM = 128
NSA_HEADS = 8
NSA_KV_GROUPS = 2
NSA_HPG = NSA_HEADS // NSA_KV_GROUPS
NSA_WIDTH = NSA_HEADS * HEAD_DIM
KV_W = NSA_KV_GROUPS * HEAD_DIM
CMP_LEN = 32
CMP_STRIDE = 16
SEL_BLOCK = 64
SEL_TOPK = 16
WINDOW = 512
Q_BLOCK = 128
HG_HEADS = 8
HG_DK = 128
HG_DV = 128
HG_WIDTH = HG_HEADS * HG_DV
CONV_CH = D_MODEL
CONV_K = 31
N_EVEN = (DEPTH + 1) // 2
DEEPNORM_ALPHA = (2.0 * DEPTH) ** 0.25
LN_EPS = 1e-5
LOG2E = 1.4426950408889634

NEG = -1e30
SEL_KEY_TILE = 2048
NSA_BATCH_PER_STEP = 1
HG_CHUNK = 128
HG_SUB = 4
HG_HEADS_PER_STEP = 8
CONV_ROWS = 128
CONV_HALO = 32
CONV_STRIP = 256
SUBLANES = 8
VMEM_LIMIT = 56 * 1024 * 1024


def _params(*sem):
    return pltpu.CompilerParams(dimension_semantics=sem, vmem_limit_bytes=VMEM_LIMIT)


def _dot(a, b):
    return jnp.dot(a, b, preferred_element_type=F32)


def _dot_nt(a, b):
    return lax.dot_general(a, b, (((1,), (1,)), ((), ())), preferred_element_type=F32)


def _dot_tn(a, b):
    return lax.dot_general(a, b, (((0,), (0,)), ((), ())), preferred_element_type=F32)


def _mm_body(x_ref, w_ref, o_ref):
    o_ref[...] = _dot(x_ref[...].astype(BF16), w_ref[...]).astype(o_ref.dtype)


def _matmul(x, w, out_dtype, bm, bn, name):
    m, k = x.shape
    n = w.shape[1]
    return pl.pallas_call(
        _mm_body,
        grid=(m // bm, n // bn),
        in_specs=[pl.BlockSpec((bm, k), lambda i, j: (i, 0)),
                  pl.BlockSpec((k, bn), lambda i, j: (0, j))],
        out_specs=pl.BlockSpec((bm, bn), lambda i, j: (i, j)),
        out_shape=jax.ShapeDtypeStruct((m, n), out_dtype),
        compiler_params=_params("parallel", "arbitrary"),
        name=name,
    )(x, w)


def _cmp_body(r_ref, pe_ref, w1_ref, w2_ref, o_ref, *, n):
    row = lax.broadcasted_iota(jnp.int32, (n, 1), 0)
    for kv in range(2):
        pe8 = jnp.broadcast_to(pe_ref[kv], (8, CMP_LEN * HEAD_DIM)).astype(BF16)
        pe_term = _dot(pe8, w1_ref[kv])[0:1, :]
        for g in range(NSA_KV_GROUPS):
            first = jnp.zeros((n, HEAD_DIM), F32)
            second = jnp.zeros((n, HEAD_DIM), F32)
            for r in range(CMP_STRIDE):
                col = r * 2 * KV_W + kv * KV_W + g * HEAD_DIM
                xr = r_ref[:, col:col + HEAD_DIM]
                first = first + _dot(xr, w1_ref[kv, r * HEAD_DIM:(r + 1) * HEAD_DIM, :])
                lo = (CMP_STRIDE + r) * HEAD_DIM
                second = second + _dot(xr, w1_ref[kv, lo:lo + HEAD_DIM, :])
            pre = first + pltpu.roll(second, n - 1, 0) + pe_term
            out = _dot(jax.nn.gelu(pre).astype(BF16), w2_ref[kv])
            o_ref[kv, g] = jnp.where(row < n - 1, out, 0.0).astype(o_ref.dtype)


def _compress(pc, pe, w1, w2):
    b, t, _ = pc.shape
    n = t // CMP_STRIDE
    r = pc.reshape(b, n, CMP_STRIDE * 2 * KV_W)
    return pl.pallas_call(
        functools.partial(_cmp_body, n=n),
        grid=(b,),
        in_specs=[pl.BlockSpec((None, n, CMP_STRIDE * 2 * KV_W), lambda i: (i, 0, 0)),
                  pl.BlockSpec((2, 1, CMP_LEN * HEAD_DIM), lambda i: (0, 0, 0)),
                  pl.BlockSpec((2, CMP_LEN * HEAD_DIM, HEAD_DIM), lambda i: (0, 0, 0)),
                  pl.BlockSpec((2, HEAD_DIM, HEAD_DIM), lambda i: (0, 0, 0))],
        out_specs=pl.BlockSpec((None, 2, NSA_KV_GROUPS, n, HEAD_DIM), lambda i: (i, 0, 0, 0, 0)),
        out_shape=jax.ShapeDtypeStruct((b, 2, NSA_KV_GROUPS, n, HEAD_DIM), BF16),
        compiler_params=_params("parallel"),
        name="nsa_compress",
    )(r, pe, w1, w2)


def _masked_softmax(s, mask):
    m = jnp.max(s, axis=-1, keepdims=True)
    e = jnp.where(mask, jnp.exp(s - m), 0.0)
    l = jnp.sum(e, axis=-1, keepdims=True)
    return e / jnp.where(l > 0.0, l, 1.0)


def _nsa_body(q_ref, ks_ref, vs_ref, kw_ref, vw_ref, kc_ref, vc_ref, gate_ref, nz_ref, ov_ref,
              eh_ref, o_ref, m_ref, l_ref, acc_ref, *, t_len, nb):
    i = pl.program_id(1)
    t0 = i * Q_BLOCK
    rows = NSA_HPG * Q_BLOCK
    n_cmp = t_len // CMP_STRIDE
    n_sel = t_len // SEL_BLOCK
    kt = SEL_KEY_TILE
    streams = [(bb, g) for bb in range(nb) for g in range(NSA_KV_GROUPS)]

    def lanes(g, width):
        return slice(g * width, (g + 1) * width)

    rowi = lax.broadcasted_iota(jnp.int32, (rows, 1), 0)
    tcol = t0 + jnp.bitwise_and(rowi, Q_BLOCK - 1)
    head_slope = lambda g, h: 2.0 ** -(g * NSA_HPG + h + 1)
    slopes = [jnp.concatenate([jnp.full((Q_BLOCK, 1), head_slope(g, h), F32) for h in range(NSA_HPG)],
                              axis=0) for g in range(NSA_KV_GROUPS)]
    q4 = [jnp.concatenate([q_ref[bb, :, g * rows + h * HEAD_DIM:g * rows + (h + 1) * HEAD_DIM]
                           for h in range(NSA_HPG)], axis=0) for bb, g in streams]

    nidx = lax.broadcasted_iota(jnp.int32, (1, n_cmp), 1)
    d_c = tcol - (nidx * CMP_STRIDE + CMP_LEN - 1)
    mask_c = d_c >= 0
    d_cf = d_c.astype(F32)
    o_c, imp = [], []
    for si, (bb, g) in enumerate(streams):
        s = jnp.where(mask_c, _dot_nt(q4[si], kc_ref[bb, g]) - slopes[g] * d_cf, NEG)
        p_c = _masked_softmax(s, mask_c)
        o_c.append(_dot(p_c.astype(BF16), vc_ref[bb, g]))
        p_sum = p_c[0:Q_BLOCK]
        for h in range(1, NSA_HPG):
            p_sum = p_sum + p_c[h * Q_BLOCK:(h + 1) * Q_BLOCK]
        p_hi = p_sum.astype(BF16)
        p_lo = (p_sum - p_hi.astype(F32)).astype(BF16)
        imp.append(_dot(p_hi, ov_ref[...]) + _dot(p_lo, ov_ref[...]))

    span = WINDOW + Q_BLOCK
    start = pl.multiple_of(jnp.maximum(t0 - WINDOW, 0), Q_BLOCK)
    kpos_w = start + lax.broadcasted_iota(jnp.int32, (1, span), 1)
    d_w = tcol - kpos_w
    mask_w = (d_w >= 0) & (d_w < WINDOW)
    d_wf = d_w.astype(F32)
    o_w = []
    for si, (bb, g) in enumerate(streams):
        kw = kw_ref[bb, pl.ds(start, span), lanes(g, HEAD_DIM)]
        s = jnp.where(mask_w, _dot_nt(q4[si], kw) - slopes[g] * d_wf, NEG)
        p_w = _masked_softmax(s, mask_w)
        o_w.append(_dot(p_w.astype(BF16), vw_ref[bb, pl.ds(start, span), lanes(g, HEAD_DIM)]))

    jrow = lax.broadcasted_iota(jnp.int32, (n_sel, 1), 0)
    jf = jrow.astype(F32)
    tt = t0 + lax.broadcasted_iota(jnp.int32, (1, Q_BLOCK), 1)
    cur = tt // SEL_BLOCK
    forced = (jrow == 0) | (jrow == cur) | (jrow == cur - 1)
    future = jrow * SEL_BLOCK > tt
    work = [jnp.where(future, -1e9, jnp.where(forced, 1e9, im.T)) for im in imp]
    picked = [jnp.zeros((n_sel, Q_BLOCK), F32) for _ in streams]
    for _ in range(min(SEL_TOPK, n_sel)):
        for si in range(len(streams)):
            mx = jnp.max(work[si], axis=0, keepdims=True)
            first = jnp.min(jnp.where(work[si] == mx, jf, 1e9), axis=0, keepdims=True)
            pick = jf == first
            picked[si] = jnp.where(pick, 1.0, picked[si])
            work[si] = jnp.where(pick, -3e38, work[si])
    sel = [jnp.where(future, 0.0, pk).T > 0.5 for pk in picked]

    jidx = lax.broadcasted_iota(jnp.int32, (1, n_sel), 1)
    blk_off = jidx.astype(F32) * float(SEL_BLOCK)
    q_aug = []
    for si, (bb, g) in enumerate(streams):
        parts = []
        for h in range(NSA_HPG):
            bias_h = jnp.where(jidx == 0, head_slope(g, h),
                               jnp.where(sel[si], head_slope(g, h) * blk_off, NEG)).astype(BF16)
            parts.append(jnp.concatenate([q4[si][h * Q_BLOCK:(h + 1) * Q_BLOCK], bias_h], axis=1))
        q_aug.append(jnp.concatenate(parts, axis=0))
        m_ref[si] = jnp.full((rows, 1), NEG, F32)
        l_ref[si] = jnp.zeros((rows, 1), F32)
        acc_ref[si] = jnp.zeros((rows, HEAD_DIM), F32)

    def sel_tile(c, causal):
        k0 = pl.multiple_of(c * kt, kt)
        eh = eh_ref[pl.ds(k0, kt), :]
        for si, (bb, g) in enumerate(streams):
            k_aug = jnp.concatenate([ks_ref[bb, pl.ds(k0, kt), lanes(g, HEAD_DIM)], eh], axis=1)
            sc = _dot_nt(q_aug[si], k_aug)
            if causal:
                kpos = k0 + lax.broadcasted_iota(jnp.int32, (1, kt), 1)
                sc = jnp.where(kpos <= tcol, sc, NEG)
            m_old = m_ref[si]
            m_new = jnp.maximum(m_old, jnp.max(sc, axis=-1, keepdims=True))
            alpha = jnp.exp(m_old - m_new)
            p = jnp.exp(sc - m_new)
            l_ref[si] = alpha * l_ref[si] + jnp.sum(p, axis=-1, keepdims=True)
            acc_ref[si] = alpha * acc_ref[si] + _dot(p.astype(BF16),
                                                     vs_ref[bb, pl.ds(k0, kt), lanes(g, HEAD_DIM)])
            m_ref[si] = m_new

    c_last = t0 // kt

    def past_tile(c, carry):
        sel_tile(c, False)
        return carry

    lax.fori_loop(0, c_last, past_tile, 0)
    sel_tile(c_last, True)

    for si, (bb, g) in enumerate(streams):
        o_s = acc_ref[si] / l_ref[si]
        gt = jax.nn.sigmoid(gate_ref[bb, :, lanes(g, HEAD_DIM)])
        outs = []
        for h in range(NSA_HPG):
            r0, r1 = h * Q_BLOCK, (h + 1) * Q_BLOCK
            a = (gt[:, 3 * h:3 * h + 1] * o_c[si][r0:r1] + gt[:, 3 * h + 1:3 * h + 2] * o_s[r0:r1]
                 + gt[:, 3 * h + 2:3 * h + 3] * o_w[si][r0:r1])
            c0 = g * rows + h * HEAD_DIM
            outs.append(a * jax.nn.silu(nz_ref[bb, :, c0:c0 + HEAD_DIM]))
        o_ref[bb, :, lanes(g, rows)] = jnp.concatenate(outs, axis=1).astype(o_ref.dtype)


def _nsa_constants(t_len):
    n_cmp = t_len // CMP_STRIDE
    n_sel = t_len // SEL_BLOCK
    s = np.arange(n_cmp)[:, None] * CMP_STRIDE
    b = np.arange(n_sel)[None, :] * SEL_BLOCK
    ov = np.clip(np.minimum(s + CMP_LEN, b + SEL_BLOCK) - np.maximum(s, b), 0, None) / CMP_LEN
    ov[n_cmp - 1] = 0.0
    key_cols = (np.arange(t_len)[:, None] // SEL_BLOCK == np.arange(n_sel)[None, :]).astype(np.float32)
    key_cols[:, 0] = np.arange(t_len) % SEL_BLOCK
    return jnp.asarray(ov, BF16), jnp.asarray(key_cols, BF16)


def _nsa(p16, cmp_kv, gates, p32):
    b, t, _ = p16.shape
    nb = NSA_BATCH_PER_STEP
    n_cmp = t // CMP_STRIDE
    n_sel = t // SEL_BLOCK
    rows = NSA_HPG * Q_BLOCK
    nstreams = nb * NSA_KV_GROUPS
    ov, key_cols = _nsa_constants(t)
    kv_base = NSA_WIDTH // KV_W
    once = pl.Buffered(1)

    def kv_spec(section):
        return pl.BlockSpec((nb, t, KV_W), lambda bi, i: (bi, 0, kv_base + section), pipeline_mode=once)

    def cmp_spec(kv):
        return pl.BlockSpec((nb, None, NSA_KV_GROUPS, n_cmp, HEAD_DIM), lambda bi, i: (bi, kv, 0, 0, 0),
                            pipeline_mode=once)

    return pl.pallas_call(
        functools.partial(_nsa_body, t_len=t, nb=nb),
        grid=(b // nb, t // Q_BLOCK),
        in_specs=[
            pl.BlockSpec((nb, Q_BLOCK, NSA_WIDTH), lambda bi, i: (bi, i, 0)),
            kv_spec(0), kv_spec(1), kv_spec(2), kv_spec(3),
            cmp_spec(0), cmp_spec(1),
            pl.BlockSpec((nb, Q_BLOCK, NSA_KV_GROUPS * HEAD_DIM), lambda bi, i: (bi, i, 0)),
            pl.BlockSpec((nb, Q_BLOCK, NSA_WIDTH), lambda bi, i: (bi, i, 0)),
            pl.BlockSpec((n_cmp, n_sel), lambda bi, i: (0, 0), pipeline_mode=once),
            pl.BlockSpec((t, n_sel), lambda bi, i: (0, 0), pipeline_mode=once),
        ],
        out_specs=pl.BlockSpec((nb, Q_BLOCK, NSA_WIDTH), lambda bi, i: (bi, i, 0)),
        out_shape=jax.ShapeDtypeStruct((b, t, NSA_WIDTH), BF16),
        scratch_shapes=[pltpu.VMEM((nstreams, rows, 1), F32), pltpu.VMEM((nstreams, rows, 1), F32),
                        pltpu.VMEM((nstreams, rows, HEAD_DIM), F32)],
        compiler_params=_params("parallel", "arbitrary"),
        name="nsa_attention",
    )(p16, p16, p16, p16, p16, cmp_kv, cmp_kv, gates, p32, ov, key_cols)


def _hgrn_level_masks():
    t = np.arange(HG_CHUNK)[:, None]
    s = np.arange(HG_CHUNK)[None, :]
    masks = []
    w = HG_SUB
    while w < HG_CHUNK:
        masks.append((t // (2 * w) == s // (2 * w)) & (t % (2 * w) >= w) & (s % (2 * w) < w))
        w *= 2
    return jnp.asarray(np.stack(masks), F32)


def _hgrn_body(hq_ref, hf_ref, hi_ref, hz_ref, lb_ref, nw_ref, mk_ref, o_ref, st_ref, *, layer):
    c = HG_CHUNK

    @pl.when(pl.program_id(2) == 0)
    def _():
        st_ref[...] = jnp.zeros_like(st_ref)

    ri = lax.broadcasted_iota(jnp.int32, (c, 1), 0)
    ci = lax.broadcasted_iota(jnp.int32, (1, c), 1)
    tri = jnp.where(ci <= ri, 1.0, 0.0).astype(BF16)
    in_sub = jnp.bitwise_and(ri, HG_SUB - 1)

    for hh in range(HG_HEADS_PER_STEP):
        ln = slice(hh * HG_DK, (hh + 1) * HG_DK)
        lbraw = lb_ref[:, ln]
        ex = jnp.exp(lbraw - jnp.max(lbraw, axis=0, keepdims=True))
        sm = ex / jnp.sum(ex, axis=0, keepdims=True)
        lb = jnp.zeros((1, HG_DK), F32)
        for r in range(1, layer + 1):
            lb = lb + sm[r:r + 1]

        hf = hf_ref[:, ln]
        q = jax.nn.silu(hq_ref[:, ln])
        v = hi_ref[:, ln]
        vb = v.astype(BF16)
        sg = jax.nn.sigmoid(hf)
        f = lb + (1.0 - lb) * sg
        k = (1.0 - lb) * (1.0 - sg)
        lf = jnp.log(f) * LOG2E

        lf1 = lf.astype(BF16)
        rem = lf - lf1.astype(F32)
        lf2 = rem.astype(BF16)
        lf3 = (rem - lf2.astype(F32)).astype(BF16)
        g = _dot(tri, lf1) + _dot(tri, lf2) + _dot(tri, lf3)

        st = st_ref[hh]
        o = _dot_nt((q * jnp.exp2(g)).astype(BF16), st.astype(BF16))

        a_off = jnp.zeros((c, c), F32)
        w = HG_SUB
        level = 0
        while w < c:
            gref = jnp.concatenate(
                [jnp.broadcast_to(g[base + w - 1:base + w], (2 * w, HG_DK)) for base in range(0, c, 2 * w)],
                axis=0)
            e = jnp.exp2(-jnp.abs(g - gref))
            a_off = a_off + mk_ref[level] * _dot_nt((q * e).astype(BF16), (k * e).astype(BF16))
            w *= 2
            level += 1
        o = o + _dot(a_off.astype(BF16), vb)

        o = o + jnp.sum(q * k, axis=-1, keepdims=True) * v
        for d in range(1, HG_SUB):
            live = in_sub >= d
            dec = jnp.exp2(jnp.where(live, g - pltpu.roll(g, d, 0), 0.0))
            a_col = jnp.sum(q * (pltpu.roll(k, d, 0) * dec), axis=-1, keepdims=True)
            o = o + jnp.where(live, a_col, 0.0) * pltpu.roll(v, d, 0)

        o = o * lax.rsqrt(jnp.mean(o * o, axis=-1, keepdims=True) + LN_EPS)
        o_ref[:, ln] = (o * nw_ref[:, ln] * jax.nn.silu(hz_ref[:, ln])).astype(o_ref.dtype)

        g_last = g[c - 1:c]
        kd = (k * jnp.exp2(g_last - g)).astype(BF16)
        st_ref[hh] = st * jnp.exp2(g_last) + _dot_tn(vb, kd)


def _hgrn(p32, lb_raw, norm_w, layer):
    b, t, _ = p32.shape
    hps = HG_HEADS_PER_STEP
    wide = hps * HG_DK
    base = NSA_WIDTH // wide
    per_section = HG_WIDTH // wide
    masks = _hgrn_level_masks()

    def col_spec(section):
        return pl.BlockSpec((None, HG_CHUNK, wide), lambda bi, h, c: (bi, c, base + section * per_section + h))

    return pl.pallas_call(
        functools.partial(_hgrn_body, layer=layer),
        grid=(b, HG_HEADS // hps, t // HG_CHUNK),
        in_specs=[col_spec(0), col_spec(1), col_spec(2), col_spec(3),
                  pl.BlockSpec((N_EVEN, wide), lambda bi, h, c: (0, h)),
                  pl.BlockSpec((1, wide), lambda bi, h, c: (0, h)),
                  pl.BlockSpec(masks.shape, lambda bi, h, c: (0, 0, 0))],
        out_specs=pl.BlockSpec((None, HG_CHUNK, wide), lambda bi, h, c: (bi, c, h)),
        out_shape=jax.ShapeDtypeStruct((b, t, HG_WIDTH), BF16),
        scratch_shapes=[pltpu.VMEM((hps, HG_DV, HG_DK), F32)],
        compiler_params=_params("parallel", "parallel", "arbitrary"),
        name="hgrn2",
    )(p32, p32, p32, p32, lb_raw, norm_w, masks)


def _glu_body(x_ref, wa_ref, wb_ref, wz_ref, u_ref, sz_ref):
    x = x_ref[...].astype(BF16)
    u_ref[...] = _dot(x, wa_ref[...]) * jax.nn.sigmoid(_dot(x, wb_ref[...]))
    sz_ref[...] = jax.nn.silu(_dot(x, wz_ref[...]))


def _glu_proj(x, w, bm, bn):
    m, k = x.shape
    nb = CONV_CH // bn

    def w_spec(part):
        return pl.BlockSpec((k, bn), lambda i, j: (0, part * nb + j))

    out = jax.ShapeDtypeStruct((m, CONV_CH), F32)
    return pl.pallas_call(
        _glu_body,
        grid=(m // bm, nb),
        in_specs=[pl.BlockSpec((bm, k), lambda i, j: (i, 0)), w_spec(0), w_spec(1), w_spec(2)],
        out_specs=[pl.BlockSpec((bm, bn), lambda i, j: (i, j))] * 2,
        out_shape=[out, out],
        compiler_params=_params("parallel", "arbitrary"),
        name="conv_glu_proj",
    )(x, w, w, w)


def _conv_body(uc_ref, up_ref, sz_ref, w_ref, cb_ref, g_ref, b_ref, o_ref, buf_ref, c_ref):
    first_tile = pl.program_id(1) == 0
    buf_ref[0:CONV_HALO, :] = jnp.where(first_tile, 0.0, up_ref[...])
    buf_ref[CONV_HALO:CONV_HALO + CONV_ROWS, :] = uc_ref[...]
    lead = CONV_HALO - (CONV_K - 1)
    buf_rows = CONV_HALO + CONV_ROWS
    for cs in range(0, CONV_CH, CONV_STRIP):
        x = buf_ref[:, cs:cs + CONV_STRIP]
        acc = jnp.broadcast_to(cb_ref[:, cs:cs + CONV_STRIP], (CONV_ROWS, CONV_STRIP))
        for phase in range(SUBLANES):
            xr = x if phase == 0 else pltpu.roll(x, buf_rows - phase, 0)
            for k in range(CONV_K):
                if (lead + k) % SUBLANES == phase:
                    a = lead + k - phase
                    acc = acc + xr[a:a + CONV_ROWS] * w_ref[k:k + 1, cs:cs + CONV_STRIP]
        c_ref[:, cs:cs + CONV_STRIP] = acc
    c = c_ref[...]
    mu = jnp.mean(c, axis=-1, keepdims=True)
    d = c - mu
    var = jnp.mean(d * d, axis=-1, keepdims=True)
    cn = d * lax.rsqrt(var + LN_EPS) * g_ref[...] + b_ref[...]
    o_ref[...] = (jax.nn.silu(cn) * sz_ref[...]).astype(o_ref.dtype)


def _conv_module(u, sz, conv_w, conv_b, ln_g, ln_b):
    b, t, ch = u.shape
    halo_per_tile = CONV_ROWS // CONV_HALO
    row = lambda: pl.BlockSpec((1, ch), lambda bi, i: (0, 0))
    tile = lambda: pl.BlockSpec((None, CONV_ROWS, ch), lambda bi, i: (bi, i, 0))
    return pl.pallas_call(
        _conv_body,
        grid=(b, t // CONV_ROWS),
        in_specs=[tile(),
                  pl.BlockSpec((None, CONV_HALO, ch),
                               lambda bi, i: (bi, jnp.maximum(i * halo_per_tile - 1, 0), 0)),
                  tile(),
                  pl.BlockSpec((CONV_HALO, ch), lambda bi, i: (0, 0)),
                  row(), row(), row()],
        out_specs=tile(),
        out_shape=jax.ShapeDtypeStruct((b, t, ch), BF16),
        scratch_shapes=[pltpu.VMEM((CONV_HALO + CONV_ROWS, ch), F32), pltpu.VMEM((CONV_ROWS, ch), F32)],
        compiler_params=_params("parallel", "arbitrary"),
        name="conv_module",
    )(u, u, sz, conv_w, conv_b, ln_g, ln_b)


def _post_body(*refs, nparts):
    y_refs = refs[:nparts]
    w_refs = refs[nparts:2 * nparts]
    x_ref, p_ref, plw_ref, gw_ref, lg_ref, lb_ref, o32_ref, o16_ref = refs[2 * nparts:]
    y = _dot(y_refs[0][...], w_refs[0][...])
    for r in range(1, nparts):
        y = y + _dot(y_refs[r][...], w_refs[r][...])
    h = DEEPNORM_ALPHA * x_ref[...] + y
    mu = jnp.mean(h, axis=-1, keepdims=True)
    d = h - mu
    var = jnp.mean(d * d, axis=-1, keepdims=True)
    x1 = d * lax.rsqrt(var + LN_EPS) * lg_ref[...] + lb_ref[...]
    gate = jax.nn.sigmoid(_dot(x1.astype(BF16), gw_ref[...]))
    ple = _dot(p_ref[...].astype(BF16), plw_ref[...])
    out = x1 + ple * gate
    o32_ref[...] = out
    o16_ref[...] = out.astype(BF16)


def _post(y_parts, w_parts, x, p, layer, ple_w, gate_w, ln_g, ln_b, bm):
    m, d = x.shape
    nparts = len(y_parts)
    resident = lambda shape: pl.BlockSpec(shape, lambda i: (0, 0), pipeline_mode=pl.Buffered(1))
    in_specs = ([pl.BlockSpec((bm, yp.shape[1]), lambda i: (i, 0)) for yp in y_parts]
                + [resident(wp.shape) for wp in w_parts]
                + [pl.BlockSpec((bm, d), lambda i: (i, 0)),
                   pl.BlockSpec((None, bm, PLE_DIM), lambda i: (layer, i, 0)),
                   resident(ple_w.shape), resident(gate_w.shape),
                   resident((1, d)), resident((1, d))])
    return pl.pallas_call(
        functools.partial(_post_body, nparts=nparts),
        grid=(m // bm,),
        in_specs=in_specs,
        out_specs=[pl.BlockSpec((bm, d), lambda i: (i, 0))] * 2,
        out_shape=[jax.ShapeDtypeStruct((m, d), F32), jax.ShapeDtypeStruct((m, d), BF16)],
        compiler_params=_params("parallel"),
        name="outproj_deepnorm_ple",
    )(*y_parts, *w_parts, x, p, ple_w, gate_w, ln_g, ln_b)


def _even_weights(w_in, w_in16):
    offs = np.cumsum([0, NSA_WIDTH, KV_W, KV_W, KV_W, KV_W, KV_W, KV_W, NSA_HEADS * 3, NSA_WIDTH,
                      HG_WIDTH, HG_WIDTH, HG_WIDTH, HG_WIDTH])
    sec = lambda a: w_in16[:, offs[a]:offs[a + 1]]
    wq = (w_in[:, :NSA_WIDTH] * HEAD_DIM ** -0.5).astype(BF16)
    w16 = jnp.concatenate([wq, sec(3), sec(4), sec(5), sec(6)], axis=1)
    wc = jnp.concatenate([sec(1), sec(2)], axis=1)
    w32 = jnp.concatenate([sec(8), sec(9), sec(10), sec(11), sec(12)], axis=1)
    per_group = NSA_HPG * 3
    wg = sec(7)
    wg = jnp.concatenate(
        [jnp.pad(wg[:, g * per_group:(g + 1) * per_group], ((0, 0), (0, HEAD_DIM - per_group)))
         for g in range(NSA_KV_GROUPS)], axis=1)
    return w16, wc, w32, wg


def _even_layer(x16, b, t, w_in, w_in16, pe_k, w1_k, w2_k, pe_v, w1_v, w2_v, hg_norm, lb_raw, layer):
    w16, wc, w32, wg = _even_weights(w_in, w_in16)
    p16 = _matmul(x16, w16, BF16, 1024, 1024, "even_proj_qkv")
    pc = _matmul(x16, wc, BF16, 1024, 2 * KV_W, "even_proj_cmp")
    p32 = _matmul(x16, w32, F32, 1024, 1024, "even_proj_gates")
    gates = _matmul(x16, wg, F32, 1024, NSA_KV_GROUPS * HEAD_DIM, "even_proj_nsa_gates")
    pe = jnp.stack([pe_k, pe_v]).reshape(2, 1, CMP_LEN * HEAD_DIM)
    w1 = jnp.stack([w1_k, w1_v]).astype(BF16)
    w2 = jnp.stack([w2_k, w2_v]).astype(BF16)
    cmp_kv = _compress(pc.reshape(b, t, 2 * KV_W), pe, w1, w2)
    p32 = p32.reshape(b, t, -1)
    ya = _nsa(p16.reshape(b, t, -1), cmp_kv, gates.reshape(b, t, -1), p32)
    yo = _hgrn(p32, lb_raw, hg_norm.reshape(1, HG_WIDTH), layer)
    return ya.reshape(b * t, NSA_WIDTH), yo.reshape(b * t, HG_WIDTH)


def _odd_layer(x16, b, t, w_in, conv_w, conv_b, ln_g, ln_b):
    u, sz = _glu_proj(x16, w_in.astype(BF16), 1024, 512)
    conv_w = jnp.pad(conv_w, ((0, CONV_HALO - CONV_K), (0, 0)))
    y = _conv_module(u.reshape(b, t, CONV_CH), sz.reshape(b, t, CONV_CH), conv_w,
                     conv_b.reshape(1, -1), ln_g.reshape(1, -1), ln_b.reshape(1, -1))
    return y.reshape(b * t, CONV_CH)


def kernel(x, p, ev_w_in, ev_cmp_pe_k, ev_cmp_w1_k, ev_cmp_w2_k, ev_cmp_pe_v, ev_cmp_w1_v,
           ev_cmp_w2_v, ev_hg_norm, hgrn_lb, ev_w_out, od_w_in, od_conv_w, od_conv_b, od_ln_g,
           od_ln_b, od_w_out, post_ln_g, post_ln_b, ple_w, ple_gate_w):
    b, t, d = x.shape
    x32 = x.reshape(b * t, d)
    x16 = x32
    ev_w_in16 = ev_w_in.astype(BF16)
    for i in range(DEPTH):
        j = i // 2
        if i % 2 == 0:
            ya, yo = _even_layer(x16, b, t, ev_w_in[j], ev_w_in16[j], ev_cmp_pe_k[j], ev_cmp_w1_k[j], ev_cmp_w2_k[j],
                                 ev_cmp_pe_v[j], ev_cmp_w1_v[j], ev_cmp_w2_v[j], ev_hg_norm[j],
                                 hgrn_lb, j)
            w_out = ev_w_out[j].astype(BF16)
            y_parts = [ya, yo]
            w_parts = [w_out[:NSA_WIDTH], w_out[NSA_WIDTH:]]
        else:
            y_parts = [_odd_layer(x16, b, t, od_w_in[j], od_conv_w[j], od_conv_b[j], od_ln_g[j],
                                  od_ln_b[j])]
            w_parts = [od_w_out[j].astype(BF16)]
        x32, x16 = _post(y_parts, w_parts, x32, p.reshape(DEPTH, b * t, PLE_DIM), i, ple_w[i].astype(BF16),
                         ple_gate_w[i].astype(BF16), post_ln_g[i].reshape(1, d),
                         post_ln_b[i].reshape(1, d), 256)
    return x32.reshape(b, t, d)
```

```python
import functools

import numpy as np
import jax
import jax.numpy as jnp
from jax import lax
from jax.experimental import pallas as pl
from jax.experimental.pallas import tpu as pltpu

F32 = jnp.float32
BF16 = jnp.bfloat16

D_MODEL = 2048
DEPTH = 4
PLE_DIM = 256
HEAD_DIM = 128
NSA_HEADS = 8
NSA_KV_GROUPS = 2
NSA_HPG = NSA_HEADS // NSA_KV_GROUPS
NSA_WIDTH = NSA_HEADS * HEAD_DIM
KV_W = NSA_KV_GROUPS * HEAD_DIM
CMP_LEN = 32
CMP_STRIDE = 16
SEL_BLOCK = 64
SEL_TOPK = 16
WINDOW = 512
Q_BLOCK = 128
HG_HEADS = 8
HG_DK = 128
HG_DV = 128
HG_WIDTH = HG_HEADS * HG_DV
CONV_CH = D_MODEL
CONV_K = 31
N_EVEN = (DEPTH + 1) // 2
DEEPNORM_ALPHA = (2.0 * DEPTH) ** 0.25
LN_EPS = 1e-5
LOG2E = 1.4426950408889634

NEG = -1e30
SEL_KEY_TILE = 2048
NSA_BATCH_PER_STEP = 1
HG_CHUNK = 128
HG_SUB = 4
HG_HEADS_PER_STEP = 8
CONV_ROWS = 128
CONV_HALO = 32
CONV_STRIP = 256
SUBLANES = 8
VMEM_LIMIT = 56 * 1024 * 1024


def _params(*sem):
    return pltpu.CompilerParams(dimension_semantics=sem, vmem_limit_bytes=VMEM_LIMIT)


def _dot(a, b):
    return jnp.dot(a, b, preferred_element_type=F32)


def _dot_nt(a, b):
    return lax.dot_general(a, b, (((1,), (1,)), ((), ())), preferred_element_type=F32)


def _dot_tn(a, b):
    return lax.dot_general(a, b, (((0,), (0,)), ((), ())), preferred_element_type=F32)


def _mm_body(x_ref, w_ref, o_ref):
    o_ref[...] = _dot(x_ref[...].astype(BF16), w_ref[...]).astype(o_ref.dtype)


def _matmul(x, w, layer, out_dtype, bm, bn, name):
    m, k = x.shape
    n = w.shape[2]
    return pl.pallas_call(
        _mm_body,
        grid=(m // bm, n // bn),
        in_specs=[pl.BlockSpec((bm, k), lambda i, j: (i, 0)),
                  pl.BlockSpec((None, k, bn), lambda i, j: (layer, 0, j))],
        out_specs=pl.BlockSpec((bm, bn), lambda i, j: (i, j)),
        out_shape=jax.ShapeDtypeStruct((m, n), out_dtype),
        compiler_params=_params("parallel", "arbitrary"),
        name=name,
    )(x, w)


def _cmp_body(r_ref, pe_ref, w1_ref, w2_ref, o_ref, *, n):
    row = lax.broadcasted_iota(jnp.int32, (n, 1), 0)
    for kv in range(2):
        pe8 = jnp.broadcast_to(pe_ref[kv], (8, CMP_LEN * HEAD_DIM)).astype(BF16)
        pe_term = _dot(pe8, w1_ref[kv])[0:1, :]
        for g in range(NSA_KV_GROUPS):
            first = jnp.zeros((n, HEAD_DIM), F32)
            second = jnp.zeros((n, HEAD_DIM), F32)
            for r in range(CMP_STRIDE):
                col = r * 2 * KV_W + kv * KV_W + g * HEAD_DIM
                xr = r_ref[:, col:col + HEAD_DIM]
                first = first + _dot(xr, w1_ref[kv, r * HEAD_DIM:(r + 1) * HEAD_DIM, :])
                lo = (CMP_STRIDE + r) * HEAD_DIM
                second = second + _dot(xr, w1_ref[kv, lo:lo + HEAD_DIM, :])
            pre = first + pltpu.roll(second, n - 1, 0) + pe_term
            out = _dot(jax.nn.gelu(pre).astype(BF16), w2_ref[kv])
            o_ref[kv, g] = jnp.where(row < n - 1, out, 0.0).astype(o_ref.dtype)


def _compress(pc, pe, w1, w2, layer):
    b, t, _ = pc.shape
    n = t // CMP_STRIDE
    r = pc.reshape(b, n, CMP_STRIDE * 2 * KV_W)
    return pl.pallas_call(
        functools.partial(_cmp_body, n=n),
        grid=(b,),
        in_specs=[pl.BlockSpec((None, n, CMP_STRIDE * 2 * KV_W), lambda i: (i, 0, 0)),
                  pl.BlockSpec((None, 2, 1, CMP_LEN * HEAD_DIM), lambda i: (layer, 0, 0, 0)),
                  pl.BlockSpec((None, 2, CMP_LEN * HEAD_DIM, HEAD_DIM), lambda i: (layer, 0, 0, 0)),
                  pl.BlockSpec((None, 2, HEAD_DIM, HEAD_DIM), lambda i: (layer, 0, 0, 0))],
        out_specs=pl.BlockSpec((None, 2, NSA_KV_GROUPS, n, HEAD_DIM), lambda i: (i, 0, 0, 0, 0)),
        out_shape=jax.ShapeDtypeStruct((b, 2, NSA_KV_GROUPS, n, HEAD_DIM), BF16),
        compiler_params=_params("parallel"),
        name="nsa_compress",
    )(r, pe, w1, w2)


def _masked_softmax(s, mask):
    m = jnp.max(s, axis=-1, keepdims=True)
    e = jnp.where(mask, jnp.exp(s - m), 0.0)
    l = jnp.sum(e, axis=-1, keepdims=True)
    return e / jnp.where(l > 0.0, l, 1.0)


def _nsa_body(q_ref, ks_ref, vs_ref, *rest, t_len, nb):
    nwin = WINDOW // Q_BLOCK + 1
    kw_refs, vw_refs = rest[:nwin], rest[nwin:2 * nwin]
    kc_ref, vc_ref, gate_ref, nz_ref, ov_ref, eh_ref, o_ref, m_ref, l_ref, acc_ref = rest[2 * nwin:]
    i = pl.program_id(1)
    t0 = i * Q_BLOCK
    rows = NSA_HPG * Q_BLOCK
    n_cmp = t_len // CMP_STRIDE
    n_sel = t_len // SEL_BLOCK
    kt = SEL_KEY_TILE
    streams = [(bb, g) for bb in range(nb) for g in range(NSA_KV_GROUPS)]

    def lanes(g, width):
        return slice(g * width, (g + 1) * width)

    rowi = lax.broadcasted_iota(jnp.int32, (rows, 1), 0)
    tcol = t0 + jnp.bitwise_and(rowi, Q_BLOCK - 1)
    head_slope = lambda g, h: 2.0 ** -(g * NSA_HPG + h + 1)
    slopes = [jnp.concatenate([jnp.full((Q_BLOCK, 1), head_slope(g, h), F32) for h in range(NSA_HPG)],
                              axis=0) for g in range(NSA_KV_GROUPS)]
    q4 = [jnp.concatenate([q_ref[bb, :, g * rows + h * HEAD_DIM:g * rows + (h + 1) * HEAD_DIM]
                           for h in range(NSA_HPG)], axis=0) for bb, g in streams]

    nidx = lax.broadcasted_iota(jnp.int32, (1, n_cmp), 1)
    d_c = tcol - (nidx * CMP_STRIDE + CMP_LEN - 1)
    mask_c = d_c >= 0
    d_cf = d_c.astype(F32)
    o_c, imp = [], []
    for si, (bb, g) in enumerate(streams):
        s = jnp.where(mask_c, _dot_nt(q4[si], kc_ref[bb, g]) - slopes[g] * d_cf, NEG)
        p_c = _masked_softmax(s, mask_c)
        o_c.append(_dot(p_c.astype(BF16), vc_ref[bb, g]))
        p_sum = p_c[0:Q_BLOCK]
        for h in range(1, NSA_HPG):
            p_sum = p_sum + p_c[h * Q_BLOCK:(h + 1) * Q_BLOCK]
        p_hi = p_sum.astype(BF16)
        p_lo = (p_sum - p_hi.astype(F32)).astype(BF16)
        imp.append(_dot(p_hi, ov_ref[...]) + _dot(p_lo, ov_ref[...]))

    span = WINDOW + Q_BLOCK
    kpos_w = (t0 - WINDOW) + lax.broadcasted_iota(jnp.int32, (1, span), 1)
    d_w = tcol - kpos_w
    mask_w = (d_w >= 0) & (d_w < WINDOW) & (kpos_w >= 0)
    d_wf = d_w.astype(F32)
    o_w = []
    for si, (bb, g) in enumerate(streams):
        kw = jnp.concatenate([r[bb, :, lanes(g, HEAD_DIM)] for r in kw_refs], axis=0)
        vw = jnp.concatenate([r[bb, :, lanes(g, HEAD_DIM)] for r in vw_refs], axis=0)
        s = jnp.where(mask_w, _dot_nt(q4[si], kw) - slopes[g] * d_wf, NEG)
        p_w = _masked_softmax(s, mask_w)
        o_w.append(_dot(p_w.astype(BF16), vw))

    jrow = lax.broadcasted_iota(jnp.int32, (n_sel, 1), 0)
    jf = jrow.astype(F32)
    tt = t0 + lax.broadcasted_iota(jnp.int32, (1, Q_BLOCK), 1)
    cur = tt // SEL_BLOCK
    forced = (jrow == 0) | (jrow == cur) | (jrow == cur - 1)
    future = jrow * SEL_BLOCK > tt
    work = [jnp.where(future, -1e9, jnp.where(forced, 1e9, im.T)) for im in imp]
    picked = [jnp.zeros((n_sel, Q_BLOCK), F32) for _ in streams]
    for _ in range(min(SEL_TOPK, n_sel)):
        for si in range(len(streams)):
            mx = jnp.max(work[si], axis=0, keepdims=True)
            first = jnp.min(jnp.where(work[si] == mx, jf, 1e9), axis=0, keepdims=True)
            pick = jf == first
            picked[si] = jnp.where(pick, 1.0, picked[si])
            work[si] = jnp.where(pick, -3e38, work[si])
    sel = [jnp.where(future, 0.0, pk).T > 0.5 for pk in picked]

    jidx = lax.broadcasted_iota(jnp.int32, (1, n_sel), 1)
    blk_off = jidx.astype(F32) * float(SEL_BLOCK)
    q_aug = []
    for si, (bb, g) in enumerate(streams):
        parts = []
        for h in range(NSA_HPG):
            bias_h = jnp.where(jidx == 0, head_slope(g, h),
                               jnp.where(sel[si], head_slope(g, h) * blk_off, NEG)).astype(BF16)
            parts.append(jnp.concatenate([q4[si][h * Q_BLOCK:(h + 1) * Q_BLOCK], bias_h], axis=1))
        q_aug.append(jnp.concatenate(parts, axis=0))
        m_ref[si] = jnp.full((rows, 1), NEG, F32)
        l_ref[si] = jnp.zeros((rows, 1), F32)
        acc_ref[si] = jnp.zeros((rows, HEAD_DIM), F32)

    def sel_tile(c, causal):
        k0 = pl.multiple_of(c * kt, kt)
        eh = eh_ref[pl.ds(k0, kt), :]
        for si, (bb, g) in enumerate(streams):
            k_aug = jnp.concatenate([ks_ref[bb, pl.ds(k0, kt), lanes(g, HEAD_DIM)], eh], axis=1)
            sc = _dot_nt(q_aug[si], k_aug)
            if causal:
                kpos = k0 + lax.broadcasted_iota(jnp.int32, (1, kt), 1)
                sc = jnp.where(kpos <= tcol, sc, NEG)
            m_old = m_ref[si]
            m_new = jnp.maximum(m_old, jnp.max(sc, axis=-1, keepdims=True))
            alpha = jnp.exp(m_old - m_new)
            p = jnp.exp(sc - m_new)
            l_ref[si] = alpha * l_ref[si] + jnp.sum(p, axis=-1, keepdims=True)
            acc_ref[si] = alpha * acc_ref[si] + _dot(p.astype(BF16),
                                                     vs_ref[bb, pl.ds(k0, kt), lanes(g, HEAD_DIM)])
            m_ref[si] = m_new

    c_last = t0 // kt

    def past_tile(c, carry):
        sel_tile(c, False)
        return carry

    lax.fori_loop(0, c_last, past_tile, 0)
    sel_tile(c_last, True)

    for si, (bb, g) in enumerate(streams):
        o_s = acc_ref[si] / l_ref[si]
        gt = jax.nn.sigmoid(gate_ref[bb, :, lanes(g, HEAD_DIM)])
        outs = []
        for h in range(NSA_HPG):
            r0, r1 = h * Q_BLOCK, (h + 1) * Q_BLOCK
            a = (gt[:, 3 * h:3 * h + 1] * o_c[si][r0:r1] + gt[:, 3 * h + 1:3 * h + 2] * o_s[r0:r1]
                 + gt[:, 3 * h + 2:3 * h + 3] * o_w[si][r0:r1])
            c0 = g * rows + h * HEAD_DIM
            outs.append(a * jax.nn.silu(nz_ref[bb, :, c0:c0 + HEAD_DIM]))
        o_ref[bb, :, lanes(g, rows)] = jnp.concatenate(outs, axis=1).astype(o_ref.dtype)


def _nsa_constants(t_len):
    n_cmp = t_len // CMP_STRIDE
    n_sel = t_len // SEL_BLOCK
    s = np.arange(n_cmp)[:, None] * CMP_STRIDE
    b = np.arange(n_sel)[None, :] * SEL_BLOCK
    ov = np.clip(np.minimum(s + CMP_LEN, b + SEL_BLOCK) - np.maximum(s, b), 0, None) / CMP_LEN
    ov[n_cmp - 1] = 0.0
    key_cols = (np.arange(t_len)[:, None] // SEL_BLOCK == np.arange(n_sel)[None, :]).astype(np.float32)
    key_cols[:, 0] = np.arange(t_len) % SEL_BLOCK
    return jnp.asarray(ov, BF16), jnp.asarray(key_cols, BF16)


def _nsa(p16, cmp_kv, gates, p32):
    b, t, _ = p16.shape
    nb = NSA_BATCH_PER_STEP
    n_cmp = t // CMP_STRIDE
    n_sel = t // SEL_BLOCK
    rows = NSA_HPG * Q_BLOCK
    nstreams = nb * NSA_KV_GROUPS
    ov, key_cols = _nsa_constants(t)
    kv_base = NSA_WIDTH // KV_W
    once = pl.Buffered(1)

    def kv_spec(section):
        return pl.BlockSpec((nb, t, KV_W), lambda bi, i: (bi, 0, kv_base + section), pipeline_mode=once)

    nwin = WINDOW // Q_BLOCK + 1

    def win_specs(section):
        return [pl.BlockSpec((nb, Q_BLOCK, KV_W),
                             lambda bi, i, k=k: (bi, jnp.maximum(i - (nwin - 1) + k, 0), kv_base + section))
                for k in range(nwin)]

    def cmp_spec(kv):
        return pl.BlockSpec((nb, None, NSA_KV_GROUPS, n_cmp, HEAD_DIM), lambda bi, i: (bi, kv, 0, 0, 0),
                            pipeline_mode=once)

    return pl.pallas_call(
        functools.partial(_nsa_body, t_len=t, nb=nb),
        grid=(b // nb, t // Q_BLOCK),
        in_specs=[
            pl.BlockSpec((nb, Q_BLOCK, NSA_WIDTH), lambda bi, i: (bi, i, 0)),
            kv_spec(0), kv_spec(1), *win_specs(2), *win_specs(3),
            cmp_spec(0), cmp_spec(1),
            pl.BlockSpec((nb, Q_BLOCK, NSA_KV_GROUPS * HEAD_DIM), lambda bi, i: (bi, i, 0)),
            pl.BlockSpec((nb, Q_BLOCK, NSA_WIDTH), lambda bi, i: (bi, i, 0)),
            pl.BlockSpec((n_cmp, n_sel), lambda bi, i: (0, 0), pipeline_mode=once),
            pl.BlockSpec((t, n_sel), lambda bi, i: (0, 0), pipeline_mode=once),
        ],
        out_specs=pl.BlockSpec((nb, Q_BLOCK, NSA_WIDTH), lambda bi, i: (bi, i, 0)),
        out_shape=jax.ShapeDtypeStruct((b, t, NSA_WIDTH), BF16),
        scratch_shapes=[pltpu.VMEM((nstreams, rows, 1), F32), pltpu.VMEM((nstreams, rows, 1), F32),
                        pltpu.VMEM((nstreams, rows, HEAD_DIM), F32)],
        compiler_params=_params("parallel", "arbitrary"),
        name="nsa_attention",
    )(*([p16] * (3 + 2 * nwin)), cmp_kv, cmp_kv, gates, p32, ov, key_cols)


def _hgrn_level_masks():
    t = np.arange(HG_CHUNK)[:, None]
    s = np.arange(HG_CHUNK)[None, :]
    masks = []
    w = HG_SUB
    while w < HG_CHUNK:
        masks.append((t // (2 * w) == s // (2 * w)) & (t % (2 * w) >= w) & (s % (2 * w) < w))
        w *= 2
    return jnp.asarray(np.stack(masks), F32)


def _hgrn_body(hq_ref, hf_ref, hi_ref, hz_ref, lb_ref, nw_ref, mk_ref, o_ref, st_ref, *, layer):
    c = HG_CHUNK

    @pl.when(pl.program_id(2) == 0)
    def _():
        st_ref[...] = jnp.zeros_like(st_ref)

    ri = lax.broadcasted_iota(jnp.int32, (c, 1), 0)
    ci = lax.broadcasted_iota(jnp.int32, (1, c), 1)
    tri = jnp.where(ci <= ri, 1.0, 0.0).astype(BF16)
    in_sub = jnp.bitwise_and(ri, HG_SUB - 1)

    for hh in range(HG_HEADS_PER_STEP):
        ln = slice(hh * HG_DK, (hh + 1) * HG_DK)
        lbraw = lb_ref[:, ln]
        ex = jnp.exp(lbraw - jnp.max(lbraw, axis=0, keepdims=True))
        sm = ex / jnp.sum(ex, axis=0, keepdims=True)
        lb = jnp.zeros((1, HG_DK), F32)
        for r in range(1, layer + 1):
            lb = lb + sm[r:r + 1]

        hf = hf_ref[:, ln]
        q = jax.nn.silu(hq_ref[:, ln])
        v = hi_ref[:, ln]
        vb = v.astype(BF16)
        sg = jax.nn.sigmoid(hf)
        f = lb + (1.0 - lb) * sg
        k = (1.0 - lb) * (1.0 - sg)
        lf = jnp.log(f) * LOG2E

        lf1 = lf.astype(BF16)
        rem = lf - lf1.astype(F32)
        lf2 = rem.astype(BF16)
        lf3 = (rem - lf2.astype(F32)).astype(BF16)
        g = _dot(tri, lf1) + _dot(tri, lf2) + _dot(tri, lf3)

        st = st_ref[hh]
        o = _dot_nt((q * jnp.exp2(g)).astype(BF16), st.astype(BF16))

        a_off = jnp.zeros((c, c), F32)
        w = HG_SUB
        level = 0
        while w < c:
            gref = jnp.concatenate(
                [jnp.broadcast_to(g[base + w - 1:base + w], (2 * w, HG_DK)) for base in range(0, c, 2 * w)],
                axis=0)
            e = jnp.exp2(-jnp.abs(g - gref))
            a_off = a_off + mk_ref[level] * _dot_nt((q * e).astype(BF16), (k * e).astype(BF16))
            w *= 2
            level += 1
        o = o + _dot(a_off.astype(BF16), vb)

        o = o + jnp.sum(q * k, axis=-1, keepdims=True) * v
        for d in range(1, HG_SUB):
            live = in_sub >= d
            dec = jnp.exp2(jnp.where(live, g - pltpu.roll(g, d, 0), 0.0))
            a_col = jnp.sum(q * (pltpu.roll(k, d, 0) * dec), axis=-1, keepdims=True)
            o = o + jnp.where(live, a_col, 0.0) * pltpu.roll(v, d, 0)

        o = o * lax.rsqrt(jnp.mean(o * o, axis=-1, keepdims=True) + LN_EPS)
        o_ref[:, ln] = (o * nw_ref[:, ln] * jax.nn.silu(hz_ref[:, ln])).astype(o_ref.dtype)

        g_last = g[c - 1:c]
        kd = (k * jnp.exp2(g_last - g)).astype(BF16)
        st_ref[hh] = st * jnp.exp2(g_last) + _dot_tn(vb, kd)


def _hgrn(p32, lb_raw, norm_w, layer):
    b, t, _ = p32.shape
    hps = HG_HEADS_PER_STEP
    wide = hps * HG_DK
    base = NSA_WIDTH // wide
    per_section = HG_WIDTH // wide
    masks = _hgrn_level_masks()

    def col_spec(section):
        return pl.BlockSpec((None, HG_CHUNK, wide), lambda bi, h, c: (bi, c, base + section * per_section + h))

    return pl.pallas_call(
        functools.partial(_hgrn_body, layer=layer),
        grid=(b, HG_HEADS // hps, t // HG_CHUNK),
        in_specs=[col_spec(0), col_spec(1), col_spec(2), col_spec(3),
                  pl.BlockSpec((N_EVEN, wide), lambda bi, h, c: (0, h)),
                  pl.BlockSpec((None, 1, wide), lambda bi, h, c: (layer, 0, h)),
                  pl.BlockSpec(masks.shape, lambda bi, h, c: (0, 0, 0))],
        out_specs=pl.BlockSpec((None, HG_CHUNK, wide), lambda bi, h, c: (bi, c, h)),
        out_shape=jax.ShapeDtypeStruct((b, t, HG_WIDTH), BF16),
        scratch_shapes=[pltpu.VMEM((hps, HG_DV, HG_DK), F32)],
        compiler_params=_params("parallel", "parallel", "arbitrary"),
        name="hgrn2",
    )(p32, p32, p32, p32, lb_raw, norm_w, masks)


def _glu_body(x_ref, wa_ref, wb_ref, wz_ref, u_ref, sz_ref):
    x = x_ref[...].astype(BF16)
    u_ref[...] = _dot(x, wa_ref[...]) * jax.nn.sigmoid(_dot(x, wb_ref[...]))
    sz_ref[...] = jax.nn.silu(_dot(x, wz_ref[...]))


def _glu_proj(x, w, layer, bm, bn):
    m, k = x.shape
    nb = CONV_CH // bn

    def w_spec(part):
        return pl.BlockSpec((None, k, bn), lambda i, j: (layer, 0, part * nb + j))

    out = jax.ShapeDtypeStruct((m, CONV_CH), F32)
    return pl.pallas_call(
        _glu_body,
        grid=(m // bm, nb),
        in_specs=[pl.BlockSpec((bm, k), lambda i, j: (i, 0)), w_spec(0), w_spec(1), w_spec(2)],
        out_specs=[pl.BlockSpec((bm, bn), lambda i, j: (i, j))] * 2,
        out_shape=[out, out],
        compiler_params=_params("parallel", "arbitrary"),
        name="conv_glu_proj",
    )(x, w, w, w)


def _conv_body(uc_ref, up_ref, sz_ref, w_ref, cb_ref, g_ref, b_ref, o_ref, buf_ref, c_ref):
    first_tile = pl.program_id(1) == 0
    buf_ref[0:CONV_HALO, :] = jnp.where(first_tile, 0.0, up_ref[...])
    buf_ref[CONV_HALO:CONV_HALO + CONV_ROWS, :] = uc_ref[...]
    lead = CONV_HALO - (CONV_K - 1)
    buf_rows = CONV_HALO + CONV_ROWS
    for cs in range(0, CONV_CH, CONV_STRIP):
        x = buf_ref[:, cs:cs + CONV_STRIP]
        acc = jnp.broadcast_to(cb_ref[:, cs:cs + CONV_STRIP], (CONV_ROWS, CONV_STRIP))
        for phase in range(SUBLANES):
            xr = x if phase == 0 else pltpu.roll(x, buf_rows - phase, 0)
            for k in range(CONV_K):
                if (lead + k) % SUBLANES == phase:
                    a = lead + k - phase
                    acc = acc + xr[a:a + CONV_ROWS] * w_ref[k:k + 1, cs:cs + CONV_STRIP]
        c_ref[:, cs:cs + CONV_STRIP] = acc
    c = c_ref[...]
    mu = jnp.mean(c, axis=-1, keepdims=True)
    d = c - mu
    var = jnp.mean(d * d, axis=-1, keepdims=True)
    cn = d * lax.rsqrt(var + LN_EPS) * g_ref[...] + b_ref[...]
    o_ref[...] = (jax.nn.silu(cn) * sz_ref[...]).astype(o_ref.dtype)


def _conv_module(u, sz, conv_w, conv_b, ln_g, ln_b, layer):
    b, t, ch = u.shape
    halo_per_tile = CONV_ROWS // CONV_HALO
    row = lambda: pl.BlockSpec((None, 1, ch), lambda bi, i: (layer, 0, 0))
    tile = lambda: pl.BlockSpec((None, CONV_ROWS, ch), lambda bi, i: (bi, i, 0))
    return pl.pallas_call(
        _conv_body,
        grid=(b, t // CONV_ROWS),
        in_specs=[tile(),
                  pl.BlockSpec((None, CONV_HALO, ch),
                               lambda bi, i: (bi, jnp.maximum(i * halo_per_tile - 1, 0), 0)),
                  tile(),
                  pl.BlockSpec((None, CONV_HALO, ch), lambda bi, i: (layer, 0, 0)),
                  row(), row(), row()],
        out_specs=tile(),
        out_shape=jax.ShapeDtypeStruct((b, t, ch), BF16),
        scratch_shapes=[pltpu.VMEM((CONV_HALO + CONV_ROWS, ch), F32), pltpu.VMEM((CONV_ROWS, ch), F32)],
        compiler_params=_params("parallel", "arbitrary"),
        name="conv_module",
    )(u, u, sz, conv_w, conv_b, ln_g, ln_b)


def _post_body(*refs, nparts):
    y_refs = refs[:nparts]
    w_refs = refs[nparts:2 * nparts]
    x_ref, p_ref, plw_ref, gw_ref, lg_ref, lb_ref, o32_ref, o16_ref = refs[2 * nparts:]
    y = _dot(y_refs[0][...], w_refs[0][...])
    for r in range(1, nparts):
        y = y + _dot(y_refs[r][...], w_refs[r][...])
    h = DEEPNORM_ALPHA * x_ref[...] + y
    mu = jnp.mean(h, axis=-1, keepdims=True)
    d = h - mu
    var = jnp.mean(d * d, axis=-1, keepdims=True)
    x1 = d * lax.rsqrt(var + LN_EPS) * lg_ref[...] + lb_ref[...]
    gate = jax.nn.sigmoid(_dot(x1.astype(BF16), gw_ref[...]))
    ple = _dot(p_ref[...].astype(BF16), plw_ref[...])
    out = x1 + ple * gate
    o32_ref[...] = out
    o16_ref[...] = out.astype(BF16)


def _post(y_parts, w_out, w_layer, x, p, layer, ple_w, gate_w, ln_g, ln_b, bm):
    m, d = x.shape
    nparts = len(y_parts)
    kp = y_parts[0].shape[1]
    once = pl.Buffered(1)
    in_specs = ([pl.BlockSpec((bm, kp), lambda i: (i, 0)) for _ in y_parts]
                + [pl.BlockSpec((None, kp, d), lambda i, r=r: (w_layer, r, 0), pipeline_mode=once)
                   for r in range(nparts)]
                + [pl.BlockSpec((bm, d), lambda i: (i, 0)),
                   pl.BlockSpec((None, bm, PLE_DIM), lambda i: (layer, i, 0)),
                   pl.BlockSpec((None, PLE_DIM, d), lambda i: (layer, 0, 0), pipeline_mode=once),
                   pl.BlockSpec((None, d, d), lambda i: (layer, 0, 0), pipeline_mode=once),
                   pl.BlockSpec((None, 1, d), lambda i: (layer, 0, 0), pipeline_mode=once),
                   pl.BlockSpec((None, 1, d), lambda i: (layer, 0, 0), pipeline_mode=once)])
    return pl.pallas_call(
        functools.partial(_post_body, nparts=nparts),
        grid=(m // bm,),
        in_specs=in_specs,
        out_specs=[pl.BlockSpec((bm, d), lambda i: (i, 0))] * 2,
        out_shape=[jax.ShapeDtypeStruct((m, d), F32), jax.ShapeDtypeStruct((m, d), BF16)],
        compiler_params=_params("parallel"),
        name="outproj_deepnorm_ple",
    )(*y_parts, *([w_out] * nparts), x, p, ple_w, gate_w, ln_g, ln_b)


def _even_weights(w_in):
    offs = np.cumsum([0, NSA_WIDTH, KV_W, KV_W, KV_W, KV_W, KV_W, KV_W, NSA_HEADS * 3, NSA_WIDTH,
                      HG_WIDTH, HG_WIDTH, HG_WIDTH, HG_WIDTH])
    w_in16 = w_in.astype(BF16)
    sec = lambda a: w_in16[:, :, offs[a]:offs[a + 1]]
    wq = (w_in[:, :, :NSA_WIDTH] * HEAD_DIM ** -0.5).astype(BF16)
    w16 = jnp.concatenate([wq, sec(3), sec(4), sec(5), sec(6)], axis=2)
    wc = jnp.concatenate([sec(1), sec(2)], axis=2)
    w32 = jnp.concatenate([sec(8), sec(9), sec(10), sec(11), sec(12)], axis=2)
    per_group = NSA_HPG * 3
    wg = sec(7)
    wg = jnp.concatenate(
        [jnp.pad(wg[:, :, g * per_group:(g + 1) * per_group], ((0, 0), (0, 0), (0, HEAD_DIM - per_group)))
         for g in range(NSA_KV_GROUPS)], axis=2)
    return w16, wc, w32, wg


def _even_layer(x16, b, t, weights, cmp_weights, hg_norm, lb_raw, layer):
    w16, wc, w32, wg = weights
    p16 = _matmul(x16, w16, layer, BF16, 1024, 1024, "even_proj_qkv")
    pc = _matmul(x16, wc, layer, BF16, 1024, 2 * KV_W, "even_proj_cmp")
    p32 = _matmul(x16, w32, layer, F32, 1024, 1024, "even_proj_gates")
    gates = _matmul(x16, wg, layer, F32, 1024, NSA_KV_GROUPS * HEAD_DIM, "even_proj_nsa_gates")
    cmp_kv = _compress(pc.reshape(b, t, 2 * KV_W), *cmp_weights, layer)
    p32 = p32.reshape(b, t, -1)
    ya = _nsa(p16.reshape(b, t, -1), cmp_kv, gates.reshape(b, t, -1), p32)
    yo = _hgrn(p32, lb_raw, hg_norm, layer)
    return ya.reshape(b * t, NSA_WIDTH), yo.reshape(b * t, HG_WIDTH)


def _odd_layer(x16, b, t, w_in16, conv_w, conv_b, ln_g, ln_b, layer):
    u, sz = _glu_proj(x16, w_in16, layer, 1024, 512)
    y = _conv_module(u.reshape(b, t, CONV_CH), sz.reshape(b, t, CONV_CH), conv_w, conv_b, ln_g, ln_b, layer)
    return y.reshape(b * t, CONV_CH)


def kernel(x, p, ev_w_in, ev_cmp_pe_k, ev_cmp_w1_k, ev_cmp_w2_k, ev_cmp_pe_v, ev_cmp_w1_v,
           ev_cmp_w2_v, ev_hg_norm, hgrn_lb, ev_w_out, od_w_in, od_conv_w, od_conv_b, od_ln_g,
           od_ln_b, od_w_out, post_ln_g, post_ln_b, ple_w, ple_gate_w):
    b, t, d = x.shape
    x32 = x.reshape(b * t, d)
    x16 = x32.astype(BF16)
    even_w = _even_weights(ev_w_in)
    n_even = ev_w_in.shape[0]
    cmp_w = (jnp.stack([ev_cmp_pe_k, ev_cmp_pe_v], axis=1).reshape(n_even, 2, 1, CMP_LEN * HEAD_DIM),
             jnp.stack([ev_cmp_w1_k, ev_cmp_w1_v], axis=1).astype(BF16),
             jnp.stack([ev_cmp_w2_k, ev_cmp_w2_v], axis=1).astype(BF16))
    hg_norm = ev_hg_norm.reshape(n_even, 1, HG_WIDTH)
    ev_w_out16 = ev_w_out.astype(BF16)
    od_w_in16 = od_w_in.astype(BF16)
    od_w_out16 = od_w_out.astype(BF16)
    n_odd = od_w_in.shape[0]
    conv_w = jnp.pad(od_conv_w, ((0, 0), (0, CONV_HALO - CONV_K), (0, 0)))
    conv_b = od_conv_b.reshape(n_odd, 1, CONV_CH)
    conv_g = od_ln_g.reshape(n_odd, 1, CONV_CH)
    conv_beta = od_ln_b.reshape(n_odd, 1, CONV_CH)
    p_all = p.reshape(DEPTH, b * t, PLE_DIM)
    ple_w16 = ple_w.astype(BF16)
    gate_w16 = ple_gate_w.astype(BF16)
    ln_g = post_ln_g.reshape(DEPTH, 1, d)
    ln_b = post_ln_b.reshape(DEPTH, 1, d)
    for i in range(DEPTH):
        j = i // 2
        if i % 2 == 0:
            y_parts = list(_even_layer(x16, b, t, even_w, cmp_w, hg_norm, hgrn_lb, j))
            w_out = ev_w_out16
        else:
            y_parts = [_odd_layer(x16, b, t, od_w_in16, conv_w, conv_b, conv_g, conv_beta, j)]
            w_out = od_w_out16
        x32, x16 = _post(y_parts, w_out, j, x32, p_all, i, ple_w16, gate_w16, ln_g, ln_b, 256)
    return x32.reshape(b, t, d)
```

```python
import functools

import numpy as np
import jax
import jax.numpy as jnp
from jax import lax
from jax.experimental import pallas as pl
from jax.experimental.pallas import tpu as pltpu

F32 = jnp.float32
BF16 = jnp.bfloat16

D_MODEL = 2048
DEPTH = 4
PLE_DIM = 256
HEAD_DIM = 128
NSA_HEADS = 8
NSA_KV_GROUPS = 2
NSA_HPG = NSA_HEADS // NSA_KV_GROUPS
NSA_WIDTH = NSA_HEADS * HEAD_DIM
KV_W = NSA_KV_GROUPS * HEAD_DIM
CMP_LEN = 32
CMP_STRIDE = 16
SEL_BLOCK = 64
SEL_TOPK = 16
WINDOW = 512
Q_BLOCK = 128
HG_HEADS = 8
HG_DK = 128
HG_DV = 128
HG_WIDTH = HG_HEADS * HG_DV
CONV_CH = D_MODEL
CONV_K = 31
N_EVEN = (DEPTH + 1) // 2
DEEPNORM_ALPHA = (2.0 * DEPTH) ** 0.25
LN_EPS = 1e-5
LOG2E = 1.4426950408889634

NEG = -1e30
SEL_KEY_TILE = 2048
NSA_BATCH_PER_STEP = 1
HG_CHUNK = 128
HG_SUB = 4
HG_HEADS_PER_STEP = 8
CONV_ROWS = 128
CONV_HALO = 32
CONV_STRIP = 256
SUBLANES = 8
VMEM_LIMIT = 56 * 1024 * 1024


def _params(*sem):
    return pltpu.CompilerParams(dimension_semantics=sem, vmem_limit_bytes=VMEM_LIMIT)


def _dot(a, b):
    return jnp.dot(a, b, preferred_element_type=F32)


def _dot_nt(a, b):
    return lax.dot_general(a, b, (((1,), (1,)), ((), ())), preferred_element_type=F32)


def _dot_tn(a, b):
    return lax.dot_general(a, b, (((0,), (0,)), ((), ())), preferred_element_type=F32)


def _mm_body(x_ref, w_ref, o_ref):
    o_ref[...] = _dot(x_ref[...].astype(BF16), w_ref[...]).astype(o_ref.dtype)


def _mm_cast_body(x_ref, w_ref, o_ref, x16_ref):
    xb = x_ref[...].astype(BF16)
    o_ref[...] = _dot(xb, w_ref[...]).astype(o_ref.dtype)

    @pl.when(pl.program_id(1) == 0)
    def _():
        x16_ref[...] = xb


def _matmul(x, w, layer, out_dtype, bm, bn, name, emit_x16=False):
    m, k = x.shape
    n = w.shape[2]
    out_specs = [pl.BlockSpec((bm, bn), lambda i, j: (i, j))]
    out_shape = [jax.ShapeDtypeStruct((m, n), out_dtype)]
    if emit_x16:
        out_specs.append(pl.BlockSpec((bm, k), lambda i, j: (i, 0)))
        out_shape.append(jax.ShapeDtypeStruct((m, k), BF16))
    out = pl.pallas_call(
        _mm_cast_body if emit_x16 else _mm_body,
        grid=(m // bm, n // bn),
        in_specs=[pl.BlockSpec((bm, k), lambda i, j: (i, 0)),
                  pl.BlockSpec((None, k, bn), lambda i, j: (layer, 0, j))],
        out_specs=out_specs,
        out_shape=out_shape,
        compiler_params=_params("parallel", "arbitrary"),
        name=name,
    )(x, w)
    return out if emit_x16 else out[0]


def _cmp_body(r_ref, pe_ref, w1_ref, w2_ref, o_ref, *, n):
    row = lax.broadcasted_iota(jnp.int32, (n, 1), 0)
    for kv in range(2):
        pe8 = jnp.broadcast_to(pe_ref[kv], (8, CMP_LEN * HEAD_DIM)).astype(BF16)
        pe_term = _dot(pe8, w1_ref[kv])[0:1, :]
        for g in range(NSA_KV_GROUPS):
            first = jnp.zeros((n, HEAD_DIM), F32)
            second = jnp.zeros((n, HEAD_DIM), F32)
            for r in range(CMP_STRIDE):
                col = r * 2 * KV_W + kv * KV_W + g * HEAD_DIM
                xr = r_ref[:, col:col + HEAD_DIM]
                first = first + _dot(xr, w1_ref[kv, r * HEAD_DIM:(r + 1) * HEAD_DIM, :])
                lo = (CMP_STRIDE + r) * HEAD_DIM
                second = second + _dot(xr, w1_ref[kv, lo:lo + HEAD_DIM, :])
            pre = first + pltpu.roll(second, n - 1, 0) + pe_term
            out = _dot(jax.nn.gelu(pre).astype(BF16), w2_ref[kv])
            o_ref[kv, g] = jnp.where(row < n - 1, out, 0.0).astype(o_ref.dtype)


def _mm_rowgroup_body(x_ref, w_ref, o_ref, acc_ref):
    n = w_ref.shape[1]
    acc = _dot(x_ref[...].astype(BF16), w_ref[...])
    for c in range(n // HEAD_DIM):
        acc_ref[c] = acc[:, c * HEAD_DIM:(c + 1) * HEAD_DIM]
    groups = acc_ref.shape[1] // CMP_STRIDE
    for r in range(CMP_STRIDE):
        for c in range(n // HEAD_DIM):
            lo = r * n + c * HEAD_DIM
            o_ref[:, lo:lo + HEAD_DIM] = acc_ref[c, pl.ds(r, groups, stride=CMP_STRIDE), :].astype(o_ref.dtype)


def _matmul_rowgroup(x, w, layer, bm, name):
    m, k = x.shape
    n = w.shape[2]
    return pl.pallas_call(
        _mm_rowgroup_body,
        grid=(m // bm,),
        in_specs=[pl.BlockSpec((bm, k), lambda i: (i, 0)),
                  pl.BlockSpec((None, k, n), lambda i: (layer, 0, 0))],
        out_specs=pl.BlockSpec((bm // CMP_STRIDE, CMP_STRIDE * n), lambda i: (i, 0)),
        out_shape=jax.ShapeDtypeStruct((m // CMP_STRIDE, CMP_STRIDE * n), BF16),
        scratch_shapes=[pltpu.VMEM((n // HEAD_DIM, bm, HEAD_DIM), F32)],
        compiler_params=_params("parallel"),
        name=name,
    )(x, w)


def _compress(r, pe, w1, w2, layer):
    b, n, _ = r.shape
    return pl.pallas_call(
        functools.partial(_cmp_body, n=n),
        grid=(b,),
        in_specs=[pl.BlockSpec((None, n, CMP_STRIDE * 2 * KV_W), lambda i: (i, 0, 0)),
                  pl.BlockSpec((None, 2, 1, CMP_LEN * HEAD_DIM), lambda i: (layer, 0, 0, 0)),
                  pl.BlockSpec((None, 2, CMP_LEN * HEAD_DIM, HEAD_DIM), lambda i: (layer, 0, 0, 0)),
                  pl.BlockSpec((None, 2, HEAD_DIM, HEAD_DIM), lambda i: (layer, 0, 0, 0))],
        out_specs=pl.BlockSpec((None, 2, NSA_KV_GROUPS, n, HEAD_DIM), lambda i: (i, 0, 0, 0, 0)),
        out_shape=jax.ShapeDtypeStruct((b, 2, NSA_KV_GROUPS, n, HEAD_DIM), BF16),
        compiler_params=_params("parallel"),
        name="nsa_compress",
    )(r, pe, w1, w2)


def _masked_softmax(s, mask):
    m = jnp.max(s, axis=-1, keepdims=True)
    e = jnp.where(mask, jnp.exp(s - m), 0.0)
    l = jnp.sum(e, axis=-1, keepdims=True)
    return e / jnp.where(l > 0.0, l, 1.0)


def _nsa_body(q_ref, ks_ref, vs_ref, *rest, t_len, nb):
    nwin = WINDOW // Q_BLOCK + 1
    kw_refs, vw_refs = rest[:nwin], rest[nwin:2 * nwin]
    kc_ref, vc_ref, gate_ref, nz_ref, ov_ref, eh_ref, o_ref, m_ref, l_ref, acc_ref = rest[2 * nwin:]
    i = pl.program_id(1)
    t0 = i * Q_BLOCK
    rows = NSA_HPG * Q_BLOCK
    n_cmp = t_len // CMP_STRIDE
    n_sel = t_len // SEL_BLOCK
    kt = SEL_KEY_TILE
    streams = [(bb, g) for bb in range(nb) for g in range(NSA_KV_GROUPS)]

    def lanes(g, width):
        return slice(g * width, (g + 1) * width)

    rowi = lax.broadcasted_iota(jnp.int32, (rows, 1), 0)
    tcol = t0 + jnp.bitwise_and(rowi, Q_BLOCK - 1)
    head_slope = lambda g, h: 2.0 ** -(g * NSA_HPG + h + 1)
    slopes = [jnp.concatenate([jnp.full((Q_BLOCK, 1), head_slope(g, h), F32) for h in range(NSA_HPG)],
                              axis=0) for g in range(NSA_KV_GROUPS)]
    q4 = [jnp.concatenate([q_ref[bb, :, g * rows + h * HEAD_DIM:g * rows + (h + 1) * HEAD_DIM]
                           for h in range(NSA_HPG)], axis=0) for bb, g in streams]

    nidx = lax.broadcasted_iota(jnp.int32, (1, n_cmp), 1)
    d_c = tcol - (nidx * CMP_STRIDE + CMP_LEN - 1)
    mask_c = d_c >= 0
    d_cf = d_c.astype(F32)
    o_c, imp = [], []
    for si, (bb, g) in enumerate(streams):
        s = jnp.where(mask_c, _dot_nt(q4[si], kc_ref[bb, g]) - slopes[g] * d_cf, NEG)
        p_c = _masked_softmax(s, mask_c)
        o_c.append(_dot(p_c.astype(BF16), vc_ref[bb, g]))
        p_sum = p_c[0:Q_BLOCK]
        for h in range(1, NSA_HPG):
            p_sum = p_sum + p_c[h * Q_BLOCK:(h + 1) * Q_BLOCK]
        p_hi = p_sum.astype(BF16)
        p_lo = (p_sum - p_hi.astype(F32)).astype(BF16)
        imp.append(_dot(p_hi, ov_ref[...]) + _dot(p_lo, ov_ref[...]))

    span = WINDOW + Q_BLOCK
    kpos_w = (t0 - WINDOW) + lax.broadcasted_iota(jnp.int32, (1, span), 1)
    d_w = tcol - kpos_w
    mask_w = (d_w >= 0) & (d_w < WINDOW) & (kpos_w >= 0)
    d_wf = d_w.astype(F32)
    o_w = []
    for si, (bb, g) in enumerate(streams):
        kw = jnp.concatenate([r[bb, :, lanes(g, HEAD_DIM)] for r in kw_refs], axis=0)
        vw = jnp.concatenate([r[bb, :, lanes(g, HEAD_DIM)] for r in vw_refs], axis=0)
        s = jnp.where(mask_w, _dot_nt(q4[si], kw) - slopes[g] * d_wf, NEG)
        p_w = _masked_softmax(s, mask_w)
        o_w.append(_dot(p_w.astype(BF16), vw))

    jrow = lax.broadcasted_iota(jnp.int32, (n_sel, 1), 0)
    jf = jrow.astype(F32)
    tt = t0 + lax.broadcasted_iota(jnp.int32, (1, Q_BLOCK), 1)
    cur = tt // SEL_BLOCK
    forced = (jrow == 0) | (jrow == cur) | (jrow == cur - 1)
    future = jrow * SEL_BLOCK > tt
    work = [jnp.where(future, -1e9, jnp.where(forced, 1e9, im.T)) for im in imp]
    picked = [jnp.zeros((n_sel, Q_BLOCK), F32) for _ in streams]
    for _ in range(min(SEL_TOPK, n_sel)):
        for si in range(len(streams)):
            mx = jnp.max(work[si], axis=0, keepdims=True)
            first = jnp.min(jnp.where(work[si] == mx, jf, 1e9), axis=0, keepdims=True)
            pick = jf == first
            picked[si] = jnp.where(pick, 1.0, picked[si])
            work[si] = jnp.where(pick, -3e38, work[si])
    sel = [jnp.where(future, 0.0, pk).T > 0.5 for pk in picked]

    jidx = lax.broadcasted_iota(jnp.int32, (1, n_sel), 1)
    blk_off = jidx.astype(F32) * float(SEL_BLOCK)
    q_aug = []
    for si, (bb, g) in enumerate(streams):
        parts = []
        for h in range(NSA_HPG):
            bias_h = jnp.where(jidx == 0, head_slope(g, h),
                               jnp.where(sel[si], head_slope(g, h) * blk_off, NEG)).astype(BF16)
            parts.append(jnp.concatenate([q4[si][h * Q_BLOCK:(h + 1) * Q_BLOCK], bias_h], axis=1))
        q_aug.append(jnp.concatenate(parts, axis=0))
        m_ref[si] = jnp.full((rows, 1), NEG, F32)
        l_ref[si] = jnp.zeros((rows, 1), F32)
        acc_ref[si] = jnp.zeros((rows, HEAD_DIM), F32)

    def sel_tile(c, causal):
        k0 = pl.multiple_of(c * kt, kt)
        eh = eh_ref[pl.ds(k0, kt), :]
        for si, (bb, g) in enumerate(streams):
            k_aug = jnp.concatenate([ks_ref[bb, pl.ds(k0, kt), lanes(g, HEAD_DIM)], eh], axis=1)
            sc = _dot_nt(q_aug[si], k_aug)
            if causal:
                kpos = k0 + lax.broadcasted_iota(jnp.int32, (1, kt), 1)
                sc = jnp.where(kpos <= tcol, sc, NEG)
            m_old = m_ref[si]
            m_new = jnp.maximum(m_old, jnp.max(sc, axis=-1, keepdims=True))
            alpha = jnp.exp(m_old - m_new)
            p = jnp.exp(sc - m_new)
            l_ref[si] = alpha * l_ref[si] + jnp.sum(p, axis=-1, keepdims=True)
            acc_ref[si] = alpha * acc_ref[si] + _dot(p.astype(BF16),
                                                     vs_ref[bb, pl.ds(k0, kt), lanes(g, HEAD_DIM)])
            m_ref[si] = m_new

    c_last = t0 // kt

    def past_tile(c, carry):
        sel_tile(c, False)
        return carry

    lax.fori_loop(0, c_last, past_tile, 0)
    sel_tile(c_last, True)

    for si, (bb, g) in enumerate(streams):
        o_s = acc_ref[si] / l_ref[si]
        gt = jax.nn.sigmoid(gate_ref[bb, :, lanes(g, HEAD_DIM)])
        outs = []
        for h in range(NSA_HPG):
            r0, r1 = h * Q_BLOCK, (h + 1) * Q_BLOCK
            a = (gt[:, 3 * h:3 * h + 1] * o_c[si][r0:r1] + gt[:, 3 * h + 1:3 * h + 2] * o_s[r0:r1]
                 + gt[:, 3 * h + 2:3 * h + 3] * o_w[si][r0:r1])
            c0 = g * rows + h * HEAD_DIM
            outs.append(a * jax.nn.silu(nz_ref[bb, :, c0:c0 + HEAD_DIM]))
        o_ref[bb, :, lanes(g, rows)] = jnp.concatenate(outs, axis=1).astype(o_ref.dtype)


def _nsa_constants(t_len):
    n_cmp = t_len // CMP_STRIDE
    n_sel = t_len // SEL_BLOCK
    s = np.arange(n_cmp)[:, None] * CMP_STRIDE
    b = np.arange(n_sel)[None, :] * SEL_BLOCK
    ov = np.clip(np.minimum(s + CMP_LEN, b + SEL_BLOCK) - np.maximum(s, b), 0, None) / CMP_LEN
    ov[n_cmp - 1] = 0.0
    key_cols = (np.arange(t_len)[:, None] // SEL_BLOCK == np.arange(n_sel)[None, :]).astype(np.float32)
    key_cols[:, 0] = np.arange(t_len) % SEL_BLOCK
    return jnp.asarray(ov, BF16), jnp.asarray(key_cols, BF16)


def _nsa(p16, cmp_kv, gates, p32):
    b, t, _ = p16.shape
    nb = NSA_BATCH_PER_STEP
    n_cmp = t // CMP_STRIDE
    n_sel = t // SEL_BLOCK
    rows = NSA_HPG * Q_BLOCK
    nstreams = nb * NSA_KV_GROUPS
    ov, key_cols = _nsa_constants(t)
    kv_base = NSA_WIDTH // KV_W
    once = pl.Buffered(1)

    def kv_spec(section):
        return pl.BlockSpec((nb, t, KV_W), lambda bi, i: (bi, 0, kv_base + section), pipeline_mode=once)

    nwin = WINDOW // Q_BLOCK + 1

    def win_specs(section):
        return [pl.BlockSpec((nb, Q_BLOCK, KV_W),
                             lambda bi, i, k=k: (bi, jnp.maximum(i - (nwin - 1) + k, 0), kv_base + section))
                for k in range(nwin)]

    def cmp_spec(kv):
        return pl.BlockSpec((nb, None, NSA_KV_GROUPS, n_cmp, HEAD_DIM), lambda bi, i: (bi, kv, 0, 0, 0),
                            pipeline_mode=once)

    return pl.pallas_call(
        functools.partial(_nsa_body, t_len=t, nb=nb),
        grid=(b // nb, t // Q_BLOCK),
        in_specs=[
            pl.BlockSpec((nb, Q_BLOCK, NSA_WIDTH), lambda bi, i: (bi, i, 0)),
            kv_spec(0), kv_spec(1), *win_specs(2), *win_specs(3),
            cmp_spec(0), cmp_spec(1),
            pl.BlockSpec((nb, Q_BLOCK, NSA_KV_GROUPS * HEAD_DIM), lambda bi, i: (bi, i, 0)),
            pl.BlockSpec((nb, Q_BLOCK, NSA_WIDTH), lambda bi, i: (bi, i, 0)),
            pl.BlockSpec((n_cmp, n_sel), lambda bi, i: (0, 0), pipeline_mode=once),
            pl.BlockSpec((t, n_sel), lambda bi, i: (0, 0), pipeline_mode=once),
        ],
        out_specs=pl.BlockSpec((nb, Q_BLOCK, NSA_WIDTH), lambda bi, i: (bi, i, 0)),
        out_shape=jax.ShapeDtypeStruct((b, t, NSA_WIDTH), BF16),
        scratch_shapes=[pltpu.VMEM((nstreams, rows, 1), F32), pltpu.VMEM((nstreams, rows, 1), F32),
                        pltpu.VMEM((nstreams, rows, HEAD_DIM), F32)],
        compiler_params=_params("parallel", "arbitrary"),
        name="nsa_attention",
    )(*([p16] * (3 + 2 * nwin)), cmp_kv, cmp_kv, gates, p32, ov, key_cols)


def _hgrn_level_masks():
    t = np.arange(HG_CHUNK)[:, None]
    s = np.arange(HG_CHUNK)[None, :]
    masks = []
    w = HG_SUB
    while w < HG_CHUNK:
        masks.append((t // (2 * w) == s // (2 * w)) & (t % (2 * w) >= w) & (s % (2 * w) < w))
        w *= 2
    return jnp.asarray(np.stack(masks), F32)


def _hgrn_body(hq_ref, hf_ref, hi_ref, hz_ref, lb_ref, nw_ref, mk_ref, o_ref, st_ref, *, layer):
    c = HG_CHUNK

    @pl.when(pl.program_id(2) == 0)
    def _():
        st_ref[...] = jnp.zeros_like(st_ref)

    ri = lax.broadcasted_iota(jnp.int32, (c, 1), 0)
    ci = lax.broadcasted_iota(jnp.int32, (1, c), 1)
    tri = jnp.where(ci <= ri, 1.0, 0.0).astype(BF16)
    in_sub = jnp.bitwise_and(ri, HG_SUB - 1)

    for hh in range(HG_HEADS_PER_STEP):
        ln = slice(hh * HG_DK, (hh + 1) * HG_DK)
        lbraw = lb_ref[:, ln]
        ex = jnp.exp(lbraw - jnp.max(lbraw, axis=0, keepdims=True))
        sm = ex / jnp.sum(ex, axis=0, keepdims=True)
        lb = jnp.zeros((1, HG_DK), F32)
        for r in range(1, layer + 1):
            lb = lb + sm[r:r + 1]

        hf = hf_ref[:, ln]
        q = jax.nn.silu(hq_ref[:, ln])
        v = hi_ref[:, ln]
        vb = v.astype(BF16)
        sg = jax.nn.sigmoid(hf)
        f = lb + (1.0 - lb) * sg
        k = (1.0 - lb) * (1.0 - sg)
        lf = jnp.log(f) * LOG2E

        lf1 = lf.astype(BF16)
        rem = lf - lf1.astype(F32)
        lf2 = rem.astype(BF16)
        lf3 = (rem - lf2.astype(F32)).astype(BF16)
        g = _dot(tri, lf1) + _dot(tri, lf2) + _dot(tri, lf3)

        st = st_ref[hh]
        o = _dot_nt((q * jnp.exp2(g)).astype(BF16), st.astype(BF16))

        a_off = jnp.zeros((c, c), F32)
        w = HG_SUB
        level = 0
        while w < c:
            gref = jnp.concatenate(
                [jnp.broadcast_to(g[base + w - 1:base + w], (2 * w, HG_DK)) for base in range(0, c, 2 * w)],
                axis=0)
            e = jnp.exp2(-jnp.abs(g - gref))
            a_off = a_off + mk_ref[level] * _dot_nt((q * e).astype(BF16), (k * e).astype(BF16))
            w *= 2
            level += 1
        o = o + _dot(a_off.astype(BF16), vb)

        o = o + jnp.sum(q * k, axis=-1, keepdims=True) * v
        for d in range(1, HG_SUB):
            live = in_sub >= d
            dec = jnp.exp2(jnp.where(live, g - pltpu.roll(g, d, 0), 0.0))
            a_col = jnp.sum(q * (pltpu.roll(k, d, 0) * dec), axis=-1, keepdims=True)
            o = o + jnp.where(live, a_col, 0.0) * pltpu.roll(v, d, 0)

        o = o * lax.rsqrt(jnp.mean(o * o, axis=-1, keepdims=True) + LN_EPS)
        o_ref[:, ln] = (o * nw_ref[:, ln] * jax.nn.silu(hz_ref[:, ln])).astype(o_ref.dtype)

        g_last = g[c - 1:c]
        kd = (k * jnp.exp2(g_last - g)).astype(BF16)
        st_ref[hh] = st * jnp.exp2(g_last) + _dot_tn(vb, kd)


def _hgrn(p32, lb_raw, norm_w, layer):
    b, t, _ = p32.shape
    hps = HG_HEADS_PER_STEP
    wide = hps * HG_DK
    base = NSA_WIDTH // wide
    per_section = HG_WIDTH // wide
    masks = _hgrn_level_masks()

    def col_spec(section):
        return pl.BlockSpec((None, HG_CHUNK, wide), lambda bi, h, c: (bi, c, base + section * per_section + h))

    return pl.pallas_call(
        functools.partial(_hgrn_body, layer=layer),
        grid=(b, HG_HEADS // hps, t // HG_CHUNK),
        in_specs=[col_spec(0), col_spec(1), col_spec(2), col_spec(3),
                  pl.BlockSpec((N_EVEN, wide), lambda bi, h, c: (0, h)),
                  pl.BlockSpec((None, 1, wide), lambda bi, h, c: (layer, 0, h)),
                  pl.BlockSpec(masks.shape, lambda bi, h, c: (0, 0, 0))],
        out_specs=pl.BlockSpec((None, HG_CHUNK, wide), lambda bi, h, c: (bi, c, h)),
        out_shape=jax.ShapeDtypeStruct((b, t, HG_WIDTH), BF16),
        scratch_shapes=[pltpu.VMEM((hps, HG_DV, HG_DK), F32)],
        compiler_params=_params("parallel", "parallel", "arbitrary"),
        name="hgrn2",
    )(p32, p32, p32, p32, lb_raw, norm_w, masks)


def _glu_body(x_ref, wa_ref, wb_ref, wz_ref, u_ref, sz_ref):
    x = x_ref[...].astype(BF16)
    u_ref[...] = _dot(x, wa_ref[...]) * jax.nn.sigmoid(_dot(x, wb_ref[...]))
    sz_ref[...] = jax.nn.silu(_dot(x, wz_ref[...]))


def _glu_proj(x, w, layer, bm, bn):
    m, k = x.shape
    nb = CONV_CH // bn

    def w_spec(part):
        return pl.BlockSpec((None, k, bn), lambda i, j: (layer, 0, part * nb + j))

    out = jax.ShapeDtypeStruct((m, CONV_CH), F32)
    return pl.pallas_call(
        _glu_body,
        grid=(m // bm, nb),
        in_specs=[pl.BlockSpec((bm, k), lambda i, j: (i, 0)), w_spec(0), w_spec(1), w_spec(2)],
        out_specs=[pl.BlockSpec((bm, bn), lambda i, j: (i, j))] * 2,
        out_shape=[out, out],
        compiler_params=_params("parallel", "arbitrary"),
        name="conv_glu_proj",
    )(x, w, w, w)


def _conv_body(uc_ref, up_ref, sz_ref, w_ref, cb_ref, g_ref, b_ref, o_ref, buf_ref, c_ref):
    first_tile = pl.program_id(1) == 0
    buf_ref[0:CONV_HALO, :] = jnp.where(first_tile, 0.0, up_ref[...])
    buf_ref[CONV_HALO:CONV_HALO + CONV_ROWS, :] = uc_ref[...]
    lead = CONV_HALO - (CONV_K - 1)
    buf_rows = CONV_HALO + CONV_ROWS
    for cs in range(0, CONV_CH, CONV_STRIP):
        x = buf_ref[:, cs:cs + CONV_STRIP]
        acc = jnp.broadcast_to(cb_ref[:, cs:cs + CONV_STRIP], (CONV_ROWS, CONV_STRIP))
        for phase in range(SUBLANES):
            xr = x if phase == 0 else pltpu.roll(x, buf_rows - phase, 0)
            for k in range(CONV_K):
                if (lead + k) % SUBLANES == phase:
                    a = lead + k - phase
                    acc = acc + xr[a:a + CONV_ROWS] * w_ref[k:k + 1, cs:cs + CONV_STRIP]
        c_ref[:, cs:cs + CONV_STRIP] = acc
    c = c_ref[...]
    mu = jnp.mean(c, axis=-1, keepdims=True)
    d = c - mu
    var = jnp.mean(d * d, axis=-1, keepdims=True)
    cn = d * lax.rsqrt(var + LN_EPS) * g_ref[...] + b_ref[...]
    o_ref[...] = (jax.nn.silu(cn) * sz_ref[...]).astype(o_ref.dtype)


def _conv_module(u, sz, conv_w, conv_b, ln_g, ln_b, layer):
    b, t, ch = u.shape
    halo_per_tile = CONV_ROWS // CONV_HALO
    row = lambda: pl.BlockSpec((None, 1, ch), lambda bi, i: (layer, 0, 0))
    tile = lambda: pl.BlockSpec((None, CONV_ROWS, ch), lambda bi, i: (bi, i, 0))
    return pl.pallas_call(
        _conv_body,
        grid=(b, t // CONV_ROWS),
        in_specs=[tile(),
                  pl.BlockSpec((None, CONV_HALO, ch),
                               lambda bi, i: (bi, jnp.maximum(i * halo_per_tile - 1, 0), 0)),
                  tile(),
                  pl.BlockSpec((None, CONV_HALO, ch), lambda bi, i: (layer, 0, 0)),
                  row(), row(), row()],
        out_specs=tile(),
        out_shape=jax.ShapeDtypeStruct((b, t, ch), BF16),
        scratch_shapes=[pltpu.VMEM((CONV_HALO + CONV_ROWS, ch), F32), pltpu.VMEM((CONV_ROWS, ch), F32)],
        compiler_params=_params("parallel", "arbitrary"),
        name="conv_module",
    )(u, u, sz, conv_w, conv_b, ln_g, ln_b)


def _post_body(*refs, nparts):
    y_refs = refs[:nparts]
    w_refs = refs[nparts:2 * nparts]
    x_ref, p_ref, plw_ref, gw_ref, lg_ref, lb_ref, o32_ref, o16_ref = refs[2 * nparts:]
    y = _dot(y_refs[0][...], w_refs[0][...])
    for r in range(1, nparts):
        y = y + _dot(y_refs[r][...], w_refs[r][...])
    h = DEEPNORM_ALPHA * x_ref[...] + y
    mu = jnp.mean(h, axis=-1, keepdims=True)
    d = h - mu
    var = jnp.mean(d * d, axis=-1, keepdims=True)
    x1 = d * lax.rsqrt(var + LN_EPS) * lg_ref[...] + lb_ref[...]
    gate = jax.nn.sigmoid(_dot(x1.astype(BF16), gw_ref[...]))
    ple = _dot(p_ref[...].astype(BF16), plw_ref[...])
    out = x1 + ple * gate
    o32_ref[...] = out
    o16_ref[...] = out.astype(BF16)


def _post(y_parts, w_out, w_layer, x, p, layer, ple_w, gate_w, ln_g, ln_b, bm):
    m, d = x.shape
    nparts = len(y_parts)
    kp = y_parts[0].shape[1]
    once = pl.Buffered(1)
    in_specs = ([pl.BlockSpec((bm, kp), lambda i: (i, 0)) for _ in y_parts]
                + [pl.BlockSpec((None, kp, d), lambda i, r=r: (w_layer, r, 0), pipeline_mode=once)
                   for r in range(nparts)]
                + [pl.BlockSpec((bm, d), lambda i: (i, 0)),
                   pl.BlockSpec((None, bm, PLE_DIM), lambda i: (layer, i, 0)),
                   pl.BlockSpec((None, PLE_DIM, d), lambda i: (layer, 0, 0), pipeline_mode=once),
                   pl.BlockSpec((None, d, d), lambda i: (layer, 0, 0), pipeline_mode=once),
                   pl.BlockSpec((None, 1, d), lambda i: (layer, 0, 0), pipeline_mode=once),
                   pl.BlockSpec((None, 1, d), lambda i: (layer, 0, 0), pipeline_mode=once)])
    return pl.pallas_call(
        functools.partial(_post_body, nparts=nparts),
        grid=(m // bm,),
        in_specs=in_specs,
        out_specs=[pl.BlockSpec((bm, d), lambda i: (i, 0))] * 2,
        out_shape=[jax.ShapeDtypeStruct((m, d), F32), jax.ShapeDtypeStruct((m, d), BF16)],
        compiler_params=_params("parallel"),
        name="outproj_deepnorm_ple",
    )(*y_parts, *([w_out] * nparts), x, p, ple_w, gate_w, ln_g, ln_b)


EVEN_SECTIONS = np.cumsum([0, NSA_WIDTH, KV_W, KV_W, KV_W, KV_W, KV_W, KV_W, NSA_HEADS * 3, NSA_WIDTH,
                           HG_WIDTH, HG_WIDTH, HG_WIDTH, HG_WIDTH])
WPREP_COLS = 256
WPREP_PIECE = 128


def _even_weights_body(wt_ref, w16_ref, wc_ref, w32_ref, wg_ref):
    o = EVEN_SECTIONS

    def move(dst_ref, dst_lo, src_lo, src_hi, scale=None):
        for r in range(src_lo, src_hi, WPREP_PIECE):
            piece = wt_ref[r:r + WPREP_PIECE, :]
            if scale is not None:
                piece = piece * scale
            c = dst_lo + r - src_lo
            dst_ref[:, c:c + WPREP_PIECE] = piece.T.astype(BF16)

    move(w16_ref, 0, o[0], o[1], HEAD_DIM ** -0.5)
    move(w16_ref, NSA_WIDTH, o[3], o[7])
    move(wc_ref, 0, o[1], o[3])
    move(w32_ref, 0, o[8], o[13])
    per_group = NSA_HPG * 3
    gt = wt_ref[o[7]:o[7] + WPREP_PIECE, :].T
    lane = lax.broadcasted_iota(jnp.int32, gt.shape, 1)
    for g in range(NSA_KV_GROUPS):
        shifted = gt if g == 0 else pltpu.roll(gt, WPREP_PIECE - g * per_group, 1)
        wg_ref[:, g * HEAD_DIM:(g + 1) * HEAD_DIM] = jnp.where(lane < per_group, shifted, 0.0).astype(BF16)


def _even_weights(w_in):
    nl, k, n = w_in.shape
    widths = (NSA_WIDTH + 4 * KV_W, 2 * KV_W, NSA_WIDTH + 4 * HG_WIDTH, NSA_KV_GROUPS * HEAD_DIM)
    return pl.pallas_call(
        _even_weights_body,
        grid=(nl, k // WPREP_COLS),
        in_specs=[pl.BlockSpec((None, n, WPREP_COLS), lambda l, i: (l, 0, i))],
        out_specs=[pl.BlockSpec((None, WPREP_COLS, wd), lambda l, i: (l, i, 0)) for wd in widths],
        out_shape=[jax.ShapeDtypeStruct((nl, k, wd), BF16) for wd in widths],
        compiler_params=_params("parallel", "parallel"),
        name="even_weight_prep",
    )(jnp.swapaxes(w_in, 1, 2))


def _even_layer(x16, b, t, weights, cmp_weights, hg_norm, lb_raw, layer):
    w16, wc, w32, wg = weights
    if x16.dtype == BF16:
        p16 = _matmul(x16, w16, layer, BF16, 1024, 1024, "even_proj_qkv")
    else:
        p16, x16 = _matmul(x16, w16, layer, BF16, 1024, 1024, "even_proj_qkv_cast", emit_x16=True)
    pc = _matmul_rowgroup(x16, wc, layer, 1024, "even_proj_cmp")
    p32 = _matmul(x16, w32, layer, F32, 1024, 1024, "even_proj_gates")
    gates = _matmul(x16, wg, layer, F32, 1024, NSA_KV_GROUPS * HEAD_DIM, "even_proj_nsa_gates")
    cmp_kv = _compress(pc.reshape(b, t // CMP_STRIDE, -1), *cmp_weights, layer)
    p32 = p32.reshape(b, t, -1)
    ya = _nsa(p16.reshape(b, t, -1), cmp_kv, gates.reshape(b, t, -1), p32)
    yo = _hgrn(p32, lb_raw, hg_norm, layer)
    return ya.reshape(b * t, NSA_WIDTH), yo.reshape(b * t, HG_WIDTH)


def _odd_layer(x16, b, t, w_in16, conv_w, conv_b, ln_g, ln_b, layer):
    u, sz = _glu_proj(x16, w_in16, layer, 1024, 512)
    y = _conv_module(u.reshape(b, t, CONV_CH), sz.reshape(b, t, CONV_CH), conv_w, conv_b, ln_g, ln_b, layer)
    return y.reshape(b * t, CONV_CH)


def kernel(x, p, ev_w_in, ev_cmp_pe_k, ev_cmp_w1_k, ev_cmp_w2_k, ev_cmp_pe_v, ev_cmp_w1_v,
           ev_cmp_w2_v, ev_hg_norm, hgrn_lb, ev_w_out, od_w_in, od_conv_w, od_conv_b, od_ln_g,
           od_ln_b, od_w_out, post_ln_g, post_ln_b, ple_w, ple_gate_w):
    b, t, d = x.shape
    x32 = x.reshape(b * t, d)
    x16 = x32
    even_w = _even_weights(ev_w_in)
    n_even = ev_w_in.shape[0]
    cmp_w = (jnp.stack([ev_cmp_pe_k, ev_cmp_pe_v], axis=1).reshape(n_even, 2, 1, CMP_LEN * HEAD_DIM),
             jnp.stack([ev_cmp_w1_k, ev_cmp_w1_v], axis=1).astype(BF16),
             jnp.stack([ev_cmp_w2_k, ev_cmp_w2_v], axis=1).astype(BF16))
    hg_norm = ev_hg_norm.reshape(n_even, 1, HG_WIDTH)
    ev_w_out16 = ev_w_out.astype(BF16)
    od_w_in16 = od_w_in.astype(BF16)
    od_w_out16 = od_w_out.astype(BF16)
    n_odd = od_w_in.shape[0]
    conv_w = jnp.pad(od_conv_w, ((0, 0), (0, CONV_HALO - CONV_K), (0, 0)))
    conv_b = od_conv_b.reshape(n_odd, 1, CONV_CH)
    conv_g = od_ln_g.reshape(n_odd, 1, CONV_CH)
    conv_beta = od_ln_b.reshape(n_odd, 1, CONV_CH)
    p_all = p.reshape(DEPTH, b * t, PLE_DIM)
    ple_w16 = ple_w.astype(BF16)
    gate_w16 = ple_gate_w.astype(BF16)
    ln_g = post_ln_g.reshape(DEPTH, 1, d)
    ln_b = post_ln_b.reshape(DEPTH, 1, d)
    for i in range(DEPTH):
        j = i // 2
        if i % 2 == 0:
            y_parts = list(_even_layer(x16, b, t, even_w, cmp_w, hg_norm, hgrn_lb, j))
            w_out = ev_w_out16
        else:
            y_parts = [_odd_layer(x16, b, t, od_w_in16, conv_w, conv_b, conv_g, conv_beta, j)]
            w_out = od_w_out16
        x32, x16 = _post(y_parts, w_out, j, x32, p_all, i, ple_w16, gate_w16, ln_g, ln_b, 256)
    return x32.reshape(b, t, d)
```

```python
import functools

import numpy as np
import jax
import jax.numpy as jnp
from jax import lax
from jax.experimental import pallas as pl
from jax.experimental.pallas import tpu as pltpu

F32 = jnp.float32
BF16 = jnp.bfloat16

D_MODEL = 2048
DEPTH = 4
PLE_DIM = 256
HEAD_DIM = 128
NSA_HEADS = 8
NSA_KV_GROUPS = 2
NSA_HPG = NSA_HEADS // NSA_KV_GROUPS
NSA_WIDTH = NSA_HEADS * HEAD_DIM
KV_W = NSA_KV_GROUPS * HEAD_DIM
CMP_LEN = 32
CMP_STRIDE = 16
SEL_BLOCK = 64
SEL_TOPK = 16
WINDOW = 512
Q_BLOCK = 128
HG_HEADS = 8
HG_DK = 128
HG_DV = 128
HG_WIDTH = HG_HEADS * HG_DV
CONV_CH = D_MODEL
CONV_K = 31
N_EVEN = (DEPTH + 1) // 2
DEEPNORM_ALPHA = (2.0 * DEPTH) ** 0.25
LN_EPS = 1e-5
LOG2E = 1.4426950408889634

NEG = -1e30
SEL_KEY_TILE = 2048
NSA_BATCH_PER_STEP = 1
HG_CHUNK = 128
HG_SUB = 4
HG_HEADS_PER_STEP = 8
CONV_ROWS = 128
CONV_HALO = 32
CONV_STRIP = 256
SUBLANES = 8
VMEM_LIMIT = 56 * 1024 * 1024


def _params(*sem):
    return pltpu.CompilerParams(dimension_semantics=sem, vmem_limit_bytes=VMEM_LIMIT)


def _dot(a, b):
    return jnp.dot(a, b, preferred_element_type=F32)


def _dot_nt(a, b):
    return lax.dot_general(a, b, (((1,), (1,)), ((), ())), preferred_element_type=F32)


def _dot_tn(a, b):
    return lax.dot_general(a, b, (((0,), (0,)), ((), ())), preferred_element_type=F32)


def _mm_body(x_ref, w_ref, o_ref):
    o_ref[...] = _dot(x_ref[...].astype(BF16), w_ref[...]).astype(o_ref.dtype)


def _mm_cast_body(x_ref, w_ref, o_ref, x16_ref):
    xb = x_ref[...].astype(BF16)
    o_ref[...] = _dot(xb, w_ref[...]).astype(o_ref.dtype)

    @pl.when(pl.program_id(1) == 0)
    def _():
        x16_ref[...] = xb


def _matmul(x, w, layer, out_dtype, bm, bn, name, emit_x16=False):
    m, k = x.shape
    n = w.shape[2]
    out_specs = [pl.BlockSpec((bm, bn), lambda i, j: (i, j))]
    out_shape = [jax.ShapeDtypeStruct((m, n), out_dtype)]
    if emit_x16:
        out_specs.append(pl.BlockSpec((bm, k), lambda i, j: (i, 0)))
        out_shape.append(jax.ShapeDtypeStruct((m, k), BF16))
    out = pl.pallas_call(
        _mm_cast_body if emit_x16 else _mm_body,
        grid=(m // bm, n // bn),
        in_specs=[pl.BlockSpec((bm, k), lambda i, j: (i, 0)),
                  pl.BlockSpec((None, k, bn), lambda i, j: (layer, 0, j))],
        out_specs=out_specs,
        out_shape=out_shape,
        compiler_params=_params("parallel", "arbitrary"),
        name=name,
    )(x, w)
    return out if emit_x16 else out[0]


def _cmp_body(r_ref, pe_ref, w1_ref, w2_ref, o_ref, *, n):
    row = lax.broadcasted_iota(jnp.int32, (n, 1), 0)
    for kv in range(2):
        pe8 = jnp.broadcast_to(pe_ref[kv], (8, CMP_LEN * HEAD_DIM)).astype(BF16)
        pe_term = _dot(pe8, w1_ref[kv])[0:1, :]
        for g in range(NSA_KV_GROUPS):
            first = jnp.zeros((n, HEAD_DIM), F32)
            second = jnp.zeros((n, HEAD_DIM), F32)
            for r in range(CMP_STRIDE):
                col = r * 2 * KV_W + kv * KV_W + g * HEAD_DIM
                xr = r_ref[:, col:col + HEAD_DIM]
                first = first + _dot(xr, w1_ref[kv, r * HEAD_DIM:(r + 1) * HEAD_DIM, :])
                lo = (CMP_STRIDE + r) * HEAD_DIM
                second = second + _dot(xr, w1_ref[kv, lo:lo + HEAD_DIM, :])
            pre = first + pltpu.roll(second, n - 1, 0) + pe_term
            out = _dot(jax.nn.gelu(pre).astype(BF16), w2_ref[kv])
            o_ref[kv, g] = jnp.where(row < n - 1, out, 0.0).astype(o_ref.dtype)


def _mm_rowgroup_body(x_ref, w_ref, o_ref, acc_ref):
    n = w_ref.shape[1]
    acc = _dot(x_ref[...].astype(BF16), w_ref[...])
    for c in range(n // HEAD_DIM):
        acc_ref[c] = acc[:, c * HEAD_DIM:(c + 1) * HEAD_DIM]
    groups = acc_ref.shape[1] // CMP_STRIDE
    for r in range(CMP_STRIDE):
        for c in range(n // HEAD_DIM):
            lo = r * n + c * HEAD_DIM
            o_ref[:, lo:lo + HEAD_DIM] = acc_ref[c, pl.ds(r, groups, stride=CMP_STRIDE), :].astype(o_ref.dtype)


def _matmul_rowgroup(x, w, layer, bm, name):
    m, k = x.shape
    n = w.shape[2]
    return pl.pallas_call(
        _mm_rowgroup_body,
        grid=(m // bm,),
        in_specs=[pl.BlockSpec((bm, k), lambda i: (i, 0)),
                  pl.BlockSpec((None, k, n), lambda i: (layer, 0, 0))],
        out_specs=pl.BlockSpec((bm // CMP_STRIDE, CMP_STRIDE * n), lambda i: (i, 0)),
        out_shape=jax.ShapeDtypeStruct((m // CMP_STRIDE, CMP_STRIDE * n), BF16),
        scratch_shapes=[pltpu.VMEM((n // HEAD_DIM, bm, HEAD_DIM), F32)],
        compiler_params=_params("parallel"),
        name=name,
    )(x, w)


def _compress(r, pe, w1, w2, layer):
    b, n, _ = r.shape
    return pl.pallas_call(
        functools.partial(_cmp_body, n=n),
        grid=(b,),
        in_specs=[pl.BlockSpec((None, n, CMP_STRIDE * 2 * KV_W), lambda i: (i, 0, 0)),
                  pl.BlockSpec((None, 2, 1, CMP_LEN * HEAD_DIM), lambda i: (layer, 0, 0, 0)),
                  pl.BlockSpec((None, 2, CMP_LEN * HEAD_DIM, HEAD_DIM), lambda i: (layer, 0, 0, 0)),
                  pl.BlockSpec((None, 2, HEAD_DIM, HEAD_DIM), lambda i: (layer, 0, 0, 0))],
        out_specs=pl.BlockSpec((None, 2, NSA_KV_GROUPS, n, HEAD_DIM), lambda i: (i, 0, 0, 0, 0)),
        out_shape=jax.ShapeDtypeStruct((b, 2, NSA_KV_GROUPS, n, HEAD_DIM), BF16),
        compiler_params=_params("parallel"),
        name="nsa_compress",
    )(r, pe, w1, w2)


def _masked_softmax(s, mask):
    m = jnp.max(s, axis=-1, keepdims=True)
    e = jnp.where(mask, jnp.exp(s - m), 0.0)
    l = jnp.sum(e, axis=-1, keepdims=True)
    return e / jnp.where(l > 0.0, l, 1.0)


def _nsa_body(q_ref, ks_ref, vs_ref, *rest, t_len, nb):
    nwin = WINDOW // Q_BLOCK + 1
    kw_refs, vw_refs = rest[:nwin], rest[nwin:2 * nwin]
    kc_ref, vc_ref, gate_ref, nz_ref, ov_ref, eh_ref, o_ref, m_ref, l_ref, acc_ref = rest[2 * nwin:]
    i = pl.program_id(1)
    t0 = i * Q_BLOCK
    rows = NSA_HPG * Q_BLOCK
    n_cmp = t_len // CMP_STRIDE
    n_sel = t_len // SEL_BLOCK
    kt = SEL_KEY_TILE
    streams = [(bb, g) for bb in range(nb) for g in range(NSA_KV_GROUPS)]

    def lanes(g, width):
        return slice(g * width, (g + 1) * width)

    rowi = lax.broadcasted_iota(jnp.int32, (rows, 1), 0)
    tcol = t0 + jnp.bitwise_and(rowi, Q_BLOCK - 1)
    head_slope = lambda g, h: 2.0 ** -(g * NSA_HPG + h + 1)
    slopes = [jnp.concatenate([jnp.full((Q_BLOCK, 1), head_slope(g, h), F32) for h in range(NSA_HPG)],
                              axis=0) for g in range(NSA_KV_GROUPS)]
    q4 = [jnp.concatenate([q_ref[bb, :, g * rows + h * HEAD_DIM:g * rows + (h + 1) * HEAD_DIM]
                           for h in range(NSA_HPG)], axis=0) for bb, g in streams]

    nidx = lax.broadcasted_iota(jnp.int32, (1, n_cmp), 1)
    d_c = tcol - (nidx * CMP_STRIDE + CMP_LEN - 1)
    mask_c = d_c >= 0
    d_cf = d_c.astype(F32)
    qk_c = [_dot_nt(q4[si], kc_ref[bb, g]) for si, (bb, g) in enumerate(streams)]
    qk_w = [_dot_nt(q4[si], jnp.concatenate([r[bb, :, lanes(g, HEAD_DIM)] for r in kw_refs], axis=0))
            for si, (bb, g) in enumerate(streams)]
    o_c, imp = [], []
    for si, (bb, g) in enumerate(streams):
        s = jnp.where(mask_c, qk_c[si] - slopes[g] * d_cf, NEG)
        p_c = _masked_softmax(s, mask_c)
        o_c.append(_dot(p_c.astype(BF16), vc_ref[bb, g]))
        p_sum = p_c[0:Q_BLOCK]
        for h in range(1, NSA_HPG):
            p_sum = p_sum + p_c[h * Q_BLOCK:(h + 1) * Q_BLOCK]
        p_hi = p_sum.astype(BF16)
        p_lo = (p_sum - p_hi.astype(F32)).astype(BF16)
        imp.append(_dot(p_hi, ov_ref[...]) + _dot(p_lo, ov_ref[...]))

    span = WINDOW + Q_BLOCK
    kpos_w = (t0 - WINDOW) + lax.broadcasted_iota(jnp.int32, (1, span), 1)
    d_w = tcol - kpos_w
    mask_w = (d_w >= 0) & (d_w < WINDOW) & (kpos_w >= 0)
    d_wf = d_w.astype(F32)
    o_w = []
    for si, (bb, g) in enumerate(streams):
        vw = jnp.concatenate([r[bb, :, lanes(g, HEAD_DIM)] for r in vw_refs], axis=0)
        s = jnp.where(mask_w, qk_w[si] - slopes[g] * d_wf, NEG)
        p_w = _masked_softmax(s, mask_w)
        o_w.append(_dot(p_w.astype(BF16), vw))

    jrow = lax.broadcasted_iota(jnp.int32, (n_sel, 1), 0)
    jf = jrow.astype(F32)
    tt = t0 + lax.broadcasted_iota(jnp.int32, (1, Q_BLOCK), 1)
    cur = tt // SEL_BLOCK
    forced = (jrow == 0) | (jrow == cur) | (jrow == cur - 1)
    future = jrow * SEL_BLOCK > tt
    work = [jnp.where(future, -1e9, jnp.where(forced, 1e9, im.T)) for im in imp]
    picked = [jnp.zeros((n_sel, Q_BLOCK), F32) for _ in streams]
    for _ in range(min(SEL_TOPK, n_sel)):
        for si in range(len(streams)):
            mx = jnp.max(work[si], axis=0, keepdims=True)
            first = jnp.min(jnp.where(work[si] == mx, jf, 1e9), axis=0, keepdims=True)
            pick = jf == first
            picked[si] = jnp.where(pick, 1.0, picked[si])
            work[si] = jnp.where(pick, -3e38, work[si])
    sel = [jnp.where(future, 0.0, pk).T > 0.5 for pk in picked]

    jidx = lax.broadcasted_iota(jnp.int32, (1, n_sel), 1)
    blk_off = jidx.astype(F32) * float(SEL_BLOCK)
    q_aug = []
    for si, (bb, g) in enumerate(streams):
        parts = []
        for h in range(NSA_HPG):
            bias_h = jnp.where(jidx == 0, head_slope(g, h),
                               jnp.where(sel[si], head_slope(g, h) * blk_off, NEG)).astype(BF16)
            parts.append(jnp.concatenate([q4[si][h * Q_BLOCK:(h + 1) * Q_BLOCK], bias_h], axis=1))
        q_aug.append(jnp.concatenate(parts, axis=0))
        m_ref[si] = jnp.full((rows, 1), NEG, F32)
        l_ref[si] = jnp.zeros((rows, 1), F32)
        acc_ref[si] = jnp.zeros((rows, HEAD_DIM), F32)

    def sel_tile(c, causal):
        k0 = pl.multiple_of(c * kt, kt)
        eh = eh_ref[pl.ds(k0, kt), :]
        scores = []
        for si, (bb, g) in enumerate(streams):
            k_aug = jnp.concatenate([ks_ref[bb, pl.ds(k0, kt), lanes(g, HEAD_DIM)], eh], axis=1)
            scores.append(_dot_nt(q_aug[si], k_aug))
        for si, (bb, g) in enumerate(streams):
            sc = scores[si]
            if causal:
                kpos = k0 + lax.broadcasted_iota(jnp.int32, (1, kt), 1)
                sc = jnp.where(kpos <= tcol, sc, NEG)
            m_old = m_ref[si]
            m_new = jnp.maximum(m_old, jnp.max(sc, axis=-1, keepdims=True))
            alpha = jnp.exp(m_old - m_new)
            p = jnp.exp(sc - m_new)
            l_ref[si] = alpha * l_ref[si] + jnp.sum(p, axis=-1, keepdims=True)
            acc_ref[si] = alpha * acc_ref[si] + _dot(p.astype(BF16),
                                                     vs_ref[bb, pl.ds(k0, kt), lanes(g, HEAD_DIM)])
            m_ref[si] = m_new

    c_last = t0 // kt

    def past_tile(c, carry):
        sel_tile(c, False)
        return carry

    lax.fori_loop(0, c_last, past_tile, 0)
    sel_tile(c_last, True)

    for si, (bb, g) in enumerate(streams):
        o_s = acc_ref[si] / l_ref[si]
        gt = jax.nn.sigmoid(gate_ref[bb, :, lanes(g, HEAD_DIM)])
        outs = []
        for h in range(NSA_HPG):
            r0, r1 = h * Q_BLOCK, (h + 1) * Q_BLOCK
            a = (gt[:, 3 * h:3 * h + 1] * o_c[si][r0:r1] + gt[:, 3 * h + 1:3 * h + 2] * o_s[r0:r1]
                 + gt[:, 3 * h + 2:3 * h + 3] * o_w[si][r0:r1])
            c0 = g * rows + h * HEAD_DIM
            outs.append(a * jax.nn.silu(nz_ref[bb, :, c0:c0 + HEAD_DIM]))
        o_ref[bb, :, lanes(g, rows)] = jnp.concatenate(outs, axis=1).astype(o_ref.dtype)


def _nsa_constants(t_len):
    n_cmp = t_len // CMP_STRIDE
    n_sel = t_len // SEL_BLOCK
    s = np.arange(n_cmp)[:, None] * CMP_STRIDE
    b = np.arange(n_sel)[None, :] * SEL_BLOCK
    ov = np.clip(np.minimum(s + CMP_LEN, b + SEL_BLOCK) - np.maximum(s, b), 0, None) / CMP_LEN
    ov[n_cmp - 1] = 0.0
    key_cols = (np.arange(t_len)[:, None] // SEL_BLOCK == np.arange(n_sel)[None, :]).astype(np.float32)
    key_cols[:, 0] = np.arange(t_len) % SEL_BLOCK
    return jnp.asarray(ov, BF16), jnp.asarray(key_cols, BF16)


def _nsa(p16, cmp_kv, gates, p32):
    b, t, _ = p16.shape
    nb = NSA_BATCH_PER_STEP
    n_cmp = t // CMP_STRIDE
    n_sel = t // SEL_BLOCK
    rows = NSA_HPG * Q_BLOCK
    nstreams = nb * NSA_KV_GROUPS
    ov, key_cols = _nsa_constants(t)
    kv_base = NSA_WIDTH // KV_W
    once = pl.Buffered(1)

    def kv_spec(section):
        return pl.BlockSpec((nb, t, KV_W), lambda bi, i: (bi, 0, kv_base + section), pipeline_mode=once)

    nwin = WINDOW // Q_BLOCK + 1

    def win_specs(section):
        return [pl.BlockSpec((nb, Q_BLOCK, KV_W),
                             lambda bi, i, k=k: (bi, jnp.maximum(i - (nwin - 1) + k, 0), kv_base + section))
                for k in range(nwin)]

    def cmp_spec(kv):
        return pl.BlockSpec((nb, None, NSA_KV_GROUPS, n_cmp, HEAD_DIM), lambda bi, i: (bi, kv, 0, 0, 0),
                            pipeline_mode=once)

    return pl.pallas_call(
        functools.partial(_nsa_body, t_len=t, nb=nb),
        grid=(b // nb, t // Q_BLOCK),
        in_specs=[
            pl.BlockSpec((nb, Q_BLOCK, NSA_WIDTH), lambda bi, i: (bi, i, 0)),
            kv_spec(0), kv_spec(1), *win_specs(2), *win_specs(3),
            cmp_spec(0), cmp_spec(1),
            pl.BlockSpec((nb, Q_BLOCK, NSA_KV_GROUPS * HEAD_DIM), lambda bi, i: (bi, i, 0)),
            pl.BlockSpec((nb, Q_BLOCK, NSA_WIDTH), lambda bi, i: (bi, i, 0)),
            pl.BlockSpec((n_cmp, n_sel), lambda bi, i: (0, 0), pipeline_mode=once),
            pl.BlockSpec((t, n_sel), lambda bi, i: (0, 0), pipeline_mode=once),
        ],
        out_specs=pl.BlockSpec((nb, Q_BLOCK, NSA_WIDTH), lambda bi, i: (bi, i, 0)),
        out_shape=jax.ShapeDtypeStruct((b, t, NSA_WIDTH), BF16),
        scratch_shapes=[pltpu.VMEM((nstreams, rows, 1), F32), pltpu.VMEM((nstreams, rows, 1), F32),
                        pltpu.VMEM((nstreams, rows, HEAD_DIM), F32)],
        compiler_params=_params("parallel", "arbitrary"),
        name="nsa_attention",
    )(*([p16] * (3 + 2 * nwin)), cmp_kv, cmp_kv, gates, p32, ov, key_cols)


def _hgrn_level_masks():
    t = np.arange(HG_CHUNK)[:, None]
    s = np.arange(HG_CHUNK)[None, :]
    masks = []
    w = HG_SUB
    while w < HG_CHUNK:
        masks.append((t // (2 * w) == s // (2 * w)) & (t % (2 * w) >= w) & (s % (2 * w) < w))
        w *= 2
    return jnp.asarray(np.stack(masks), F32)


def _hgrn_body(hq_ref, hf_ref, hi_ref, hz_ref, lb_ref, nw_ref, mk_ref, o_ref, st_ref, *, layer):
    c = HG_CHUNK

    @pl.when(pl.program_id(2) == 0)
    def _():
        st_ref[...] = jnp.zeros_like(st_ref)

    ri = lax.broadcasted_iota(jnp.int32, (c, 1), 0)
    ci = lax.broadcasted_iota(jnp.int32, (1, c), 1)
    tri = jnp.where(ci <= ri, 1.0, 0.0).astype(BF16)
    in_sub = jnp.bitwise_and(ri, HG_SUB - 1)

    for hh in range(HG_HEADS_PER_STEP):
        ln = slice(hh * HG_DK, (hh + 1) * HG_DK)
        lbraw = lb_ref[:, ln]
        ex = jnp.exp(lbraw - jnp.max(lbraw, axis=0, keepdims=True))
        sm = ex / jnp.sum(ex, axis=0, keepdims=True)
        lb = jnp.zeros((1, HG_DK), F32)
        for r in range(1, layer + 1):
            lb = lb + sm[r:r + 1]

        hf = hf_ref[:, ln]
        q = jax.nn.silu(hq_ref[:, ln])
        v = hi_ref[:, ln]
        vb = v.astype(BF16)
        sg = jax.nn.sigmoid(hf)
        f = lb + (1.0 - lb) * sg
        k = (1.0 - lb) * (1.0 - sg)
        lf = jnp.log(f) * LOG2E

        lf1 = lf.astype(BF16)
        rem = lf - lf1.astype(F32)
        lf2 = rem.astype(BF16)
        lf3 = (rem - lf2.astype(F32)).astype(BF16)
        g = _dot(tri, lf1) + _dot(tri, lf2) + _dot(tri, lf3)

        st = st_ref[hh]
        o = _dot_nt((q * jnp.exp2(g)).astype(BF16), st.astype(BF16))

        a_off = jnp.zeros((c, c), F32)
        w = HG_SUB
        level = 0
        while w < c:
            gref = jnp.concatenate(
                [jnp.broadcast_to(g[base + w - 1:base + w], (2 * w, HG_DK)) for base in range(0, c, 2 * w)],
                axis=0)
            e = jnp.exp2(-jnp.abs(g - gref))
            a_off = a_off + mk_ref[level] * _dot_nt((q * e).astype(BF16), (k * e).astype(BF16))
            w *= 2
            level += 1
        o = o + _dot(a_off.astype(BF16), vb)

        o = o + jnp.sum(q * k, axis=-1, keepdims=True) * v
        for d in range(1, HG_SUB):
            live = in_sub >= d
            dec = jnp.exp2(jnp.where(live, g - pltpu.roll(g, d, 0), 0.0))
            a_col = jnp.sum(q * (pltpu.roll(k, d, 0) * dec), axis=-1, keepdims=True)
            o = o + jnp.where(live, a_col, 0.0) * pltpu.roll(v, d, 0)

        o = o * lax.rsqrt(jnp.mean(o * o, axis=-1, keepdims=True) + LN_EPS)
        o_ref[:, ln] = (o * nw_ref[:, ln] * jax.nn.silu(hz_ref[:, ln])).astype(o_ref.dtype)

        g_last = g[c - 1:c]
        kd = (k * jnp.exp2(g_last - g)).astype(BF16)
        st_ref[hh] = st * jnp.exp2(g_last) + _dot_tn(vb, kd)


def _hgrn(p32, lb_raw, norm_w, layer):
    b, t, _ = p32.shape
    hps = HG_HEADS_PER_STEP
    wide = hps * HG_DK
    base = NSA_WIDTH // wide
    per_section = HG_WIDTH // wide
    masks = _hgrn_level_masks()

    def col_spec(section):
        return pl.BlockSpec((None, HG_CHUNK, wide), lambda bi, h, c: (bi, c, base + section * per_section + h))

    return pl.pallas_call(
        functools.partial(_hgrn_body, layer=layer),
        grid=(b, HG_HEADS // hps, t // HG_CHUNK),
        in_specs=[col_spec(0), col_spec(1), col_spec(2), col_spec(3),
                  pl.BlockSpec((N_EVEN, wide), lambda bi, h, c: (0, h)),
                  pl.BlockSpec((None, 1, wide), lambda bi, h, c: (layer, 0, h)),
                  pl.BlockSpec(masks.shape, lambda bi, h, c: (0, 0, 0))],
        out_specs=pl.BlockSpec((None, HG_CHUNK, wide), lambda bi, h, c: (bi, c, h)),
        out_shape=jax.ShapeDtypeStruct((b, t, HG_WIDTH), BF16),
        scratch_shapes=[pltpu.VMEM((hps, HG_DV, HG_DK), F32)],
        compiler_params=_params("parallel", "parallel", "arbitrary"),
        name="hgrn2",
    )(p32, p32, p32, p32, lb_raw, norm_w, masks)


def _glu_body(x_ref, wa_ref, wb_ref, wz_ref, u_ref, sz_ref):
    x = x_ref[...].astype(BF16)
    u_ref[...] = _dot(x, wa_ref[...]) * jax.nn.sigmoid(_dot(x, wb_ref[...]))
    sz_ref[...] = jax.nn.silu(_dot(x, wz_ref[...]))


def _glu_proj(x, w, layer, bm, bn):
    m, k = x.shape
    nb = CONV_CH // bn

    def w_spec(part):
        return pl.BlockSpec((None, k, bn), lambda i, j: (layer, 0, part * nb + j))

    out = jax.ShapeDtypeStruct((m, CONV_CH), F32)
    return pl.pallas_call(
        _glu_body,
        grid=(m // bm, nb),
        in_specs=[pl.BlockSpec((bm, k), lambda i, j: (i, 0)), w_spec(0), w_spec(1), w_spec(2)],
        out_specs=[pl.BlockSpec((bm, bn), lambda i, j: (i, j))] * 2,
        out_shape=[out, out],
        compiler_params=_params("parallel", "arbitrary"),
        name="conv_glu_proj",
    )(x, w, w, w)


def _conv_body(uc_ref, up_ref, sz_ref, w_ref, cb_ref, g_ref, b_ref, o_ref, buf_ref, c_ref):
    first_tile = pl.program_id(1) == 0
    buf_ref[0:CONV_HALO, :] = jnp.where(first_tile, 0.0, up_ref[...])
    buf_ref[CONV_HALO:CONV_HALO + CONV_ROWS, :] = uc_ref[...]
    lead = CONV_HALO - (CONV_K - 1)
    buf_rows = CONV_HALO + CONV_ROWS
    for cs in range(0, CONV_CH, CONV_STRIP):
        x = buf_ref[:, cs:cs + CONV_STRIP]
        acc = jnp.broadcast_to(cb_ref[:, cs:cs + CONV_STRIP], (CONV_ROWS, CONV_STRIP))
        for phase in range(SUBLANES):
            xr = x if phase == 0 else pltpu.roll(x, buf_rows - phase, 0)
            for k in range(CONV_K):
                if (lead + k) % SUBLANES == phase:
                    a = lead + k - phase
                    acc = acc + xr[a:a + CONV_ROWS] * w_ref[k:k + 1, cs:cs + CONV_STRIP]
        c_ref[:, cs:cs + CONV_STRIP] = acc
    c = c_ref[...]
    mu = jnp.mean(c, axis=-1, keepdims=True)
    d = c - mu
    var = jnp.mean(d * d, axis=-1, keepdims=True)
    cn = d * lax.rsqrt(var + LN_EPS) * g_ref[...] + b_ref[...]
    o_ref[...] = (jax.nn.silu(cn) * sz_ref[...]).astype(o_ref.dtype)


def _conv_module(u, sz, conv_w, conv_b, ln_g, ln_b, layer):
    b, t, ch = u.shape
    halo_per_tile = CONV_ROWS // CONV_HALO
    row = lambda: pl.BlockSpec((None, 1, ch), lambda bi, i: (layer, 0, 0))
    tile = lambda: pl.BlockSpec((None, CONV_ROWS, ch), lambda bi, i: (bi, i, 0))
    return pl.pallas_call(
        _conv_body,
        grid=(b, t // CONV_ROWS),
        in_specs=[tile(),
                  pl.BlockSpec((None, CONV_HALO, ch),
                               lambda bi, i: (bi, jnp.maximum(i * halo_per_tile - 1, 0), 0)),
                  tile(),
                  pl.BlockSpec((None, CONV_HALO, ch), lambda bi, i: (layer, 0, 0)),
                  row(), row(), row()],
        out_specs=tile(),
        out_shape=jax.ShapeDtypeStruct((b, t, ch), BF16),
        scratch_shapes=[pltpu.VMEM((CONV_HALO + CONV_ROWS, ch), F32), pltpu.VMEM((CONV_ROWS, ch), F32)],
        compiler_params=_params("parallel", "arbitrary"),
        name="conv_module",
    )(u, u, sz, conv_w, conv_b, ln_g, ln_b)


def _post_body(*refs, nparts):
    y_refs = refs[:nparts]
    w_refs = refs[nparts:2 * nparts]
    x_ref, p_ref, plw_ref, gw_ref, lg_ref, lb_ref, o32_ref, o16_ref = refs[2 * nparts:]
    y = _dot(y_refs[0][...], w_refs[0][...])
    for r in range(1, nparts):
        y = y + _dot(y_refs[r][...], w_refs[r][...])
    h = DEEPNORM_ALPHA * x_ref[...] + y
    mu = jnp.mean(h, axis=-1, keepdims=True)
    d = h - mu
    var = jnp.mean(d * d, axis=-1, keepdims=True)
    x1 = d * lax.rsqrt(var + LN_EPS) * lg_ref[...] + lb_ref[...]
    gate = jax.nn.sigmoid(_dot(x1.astype(BF16), gw_ref[...]))
    ple = _dot(p_ref[...].astype(BF16), plw_ref[...])
    out = x1 + ple * gate
    o32_ref[...] = out
    o16_ref[...] = out.astype(BF16)


def _post(y_parts, w_out, w_layer, x, p, layer, ple_w, gate_w, ln_g, ln_b, bm):
    m, d = x.shape
    nparts = len(y_parts)
    kp = y_parts[0].shape[1]
    once = pl.Buffered(1)
    in_specs = ([pl.BlockSpec((bm, kp), lambda i: (i, 0)) for _ in y_parts]
                + [pl.BlockSpec((None, kp, d), lambda i, r=r: (w_layer, r, 0), pipeline_mode=once)
                   for r in range(nparts)]
                + [pl.BlockSpec((bm, d), lambda i: (i, 0)),
                   pl.BlockSpec((None, bm, PLE_DIM), lambda i: (layer, i, 0)),
                   pl.BlockSpec((None, PLE_DIM, d), lambda i: (layer, 0, 0), pipeline_mode=once),
                   pl.BlockSpec((None, d, d), lambda i: (layer, 0, 0), pipeline_mode=once),
                   pl.BlockSpec((None, 1, d), lambda i: (layer, 0, 0), pipeline_mode=once),
                   pl.BlockSpec((None, 1, d), lambda i: (layer, 0, 0), pipeline_mode=once)])
    return pl.pallas_call(
        functools.partial(_post_body, nparts=nparts),
        grid=(m // bm,),
        in_specs=in_specs,
        out_specs=[pl.BlockSpec((bm, d), lambda i: (i, 0))] * 2,
        out_shape=[jax.ShapeDtypeStruct((m, d), F32), jax.ShapeDtypeStruct((m, d), BF16)],
        compiler_params=_params("parallel"),
        name="outproj_deepnorm_ple",
    )(*y_parts, *([w_out] * nparts), x, p, ple_w, gate_w, ln_g, ln_b)


EVEN_SECTIONS = np.cumsum([0, NSA_WIDTH, KV_W, KV_W, KV_W, KV_W, KV_W, KV_W, NSA_HEADS * 3, NSA_WIDTH,
                           HG_WIDTH, HG_WIDTH, HG_WIDTH, HG_WIDTH])
WPREP_COLS = 256
WPREP_PIECE = 128


def _even_weights_body(wt_ref, w16_ref, wc_ref, w32_ref, wg_ref):
    o = EVEN_SECTIONS

    def move(dst_ref, dst_lo, src_lo, src_hi, scale=None):
        for r in range(src_lo, src_hi, WPREP_PIECE):
            piece = wt_ref[r:r + WPREP_PIECE, :]
            if scale is not None:
                piece = piece * scale
            c = dst_lo + r - src_lo
            dst_ref[:, c:c + WPREP_PIECE] = piece.T.astype(BF16)

    move(w16_ref, 0, o[0], o[1], HEAD_DIM ** -0.5)
    move(w16_ref, NSA_WIDTH, o[3], o[7])
    move(wc_ref, 0, o[1], o[3])
    move(w32_ref, 0, o[8], o[13])
    per_group = NSA_HPG * 3
    gt = wt_ref[o[7]:o[7] + WPREP_PIECE, :].T
    lane = lax.broadcasted_iota(jnp.int32, gt.shape, 1)
    for g in range(NSA_KV_GROUPS):
        shifted = gt if g == 0 else pltpu.roll(gt, WPREP_PIECE - g * per_group, 1)
        wg_ref[:, g * HEAD_DIM:(g + 1) * HEAD_DIM] = jnp.where(lane < per_group, shifted, 0.0).astype(BF16)


def _even_weights(w_in):
    nl, k, n = w_in.shape
    widths = (NSA_WIDTH + 4 * KV_W, 2 * KV_W, NSA_WIDTH + 4 * HG_WIDTH, NSA_KV_GROUPS * HEAD_DIM)
    return pl.pallas_call(
        _even_weights_body,
        grid=(nl, k // WPREP_COLS),
        in_specs=[pl.BlockSpec((None, n, WPREP_COLS), lambda l, i: (l, 0, i))],
        out_specs=[pl.BlockSpec((None, WPREP_COLS, wd), lambda l, i: (l, i, 0)) for wd in widths],
        out_shape=[jax.ShapeDtypeStruct((nl, k, wd), BF16) for wd in widths],
        compiler_params=_params("parallel", "parallel"),
        name="even_weight_prep",
    )(jnp.swapaxes(w_in, 1, 2))


def _even_layer(x16, b, t, weights, cmp_weights, hg_norm, lb_raw, layer):
    w16, wc, w32, wg = weights
    if x16.dtype == BF16:
        p16 = _matmul(x16, w16, layer, BF16, 1024, 1024, "even_proj_qkv")
    else:
        p16, x16 = _matmul(x16, w16, layer, BF16, 1024, 1024, "even_proj_qkv_cast", emit_x16=True)
    pc = _matmul_rowgroup(x16, wc, layer, 1024, "even_proj_cmp")
    p32 = _matmul(x16, w32, layer, F32, 1024, 1024, "even_proj_gates")
    gates = _matmul(x16, wg, layer, F32, 1024, NSA_KV_GROUPS * HEAD_DIM, "even_proj_nsa_gates")
    cmp_kv = _compress(pc.reshape(b, t // CMP_STRIDE, -1), *cmp_weights, layer)
    p32 = p32.reshape(b, t, -1)
    ya = _nsa(p16.reshape(b, t, -1), cmp_kv, gates.reshape(b, t, -1), p32)
    yo = _hgrn(p32, lb_raw, hg_norm, layer)
    return ya.reshape(b * t, NSA_WIDTH), yo.reshape(b * t, HG_WIDTH)


def _odd_layer(x16, b, t, w_in16, conv_w, conv_b, ln_g, ln_b, layer):
    u, sz = _glu_proj(x16, w_in16, layer, 1024, 512)
    y = _conv_module(u.reshape(b, t, CONV_CH), sz.reshape(b, t, CONV_CH), conv_w, conv_b, ln_g, ln_b, layer)
    return y.reshape(b * t, CONV_CH)


def kernel(x, p, ev_w_in, ev_cmp_pe_k, ev_cmp_w1_k, ev_cmp_w2_k, ev_cmp_pe_v, ev_cmp_w1_v,
           ev_cmp_w2_v, ev_hg_norm, hgrn_lb, ev_w_out, od_w_in, od_conv_w, od_conv_b, od_ln_g,
           od_ln_b, od_w_out, post_ln_g, post_ln_b, ple_w, ple_gate_w):
    b, t, d = x.shape
    x32 = x.reshape(b * t, d)
    x16 = x32
    even_w = _even_weights(ev_w_in)
    n_even = ev_w_in.shape[0]
    cmp_w = (jnp.stack([ev_cmp_pe_k, ev_cmp_pe_v], axis=1).reshape(n_even, 2, 1, CMP_LEN * HEAD_DIM),
             jnp.stack([ev_cmp_w1_k, ev_cmp_w1_v], axis=1).astype(BF16),
             jnp.stack([ev_cmp_w2_k, ev_cmp_w2_v], axis=1).astype(BF16))
    hg_norm = ev_hg_norm.reshape(n_even, 1, HG_WIDTH)
    ev_w_out16 = ev_w_out.astype(BF16)
    od_w_in16 = od_w_in.astype(BF16)
    od_w_out16 = od_w_out.astype(BF16)
    n_odd = od_w_in.shape[0]
    conv_w = jnp.pad(od_conv_w, ((0, 0), (0, CONV_HALO - CONV_K), (0, 0)))
    conv_b = od_conv_b.reshape(n_odd, 1, CONV_CH)
    conv_g = od_ln_g.reshape(n_odd, 1, CONV_CH)
    conv_beta = od_ln_b.reshape(n_odd, 1, CONV_CH)
    p_all = p.reshape(DEPTH, b * t, PLE_DIM)
    ple_w16 = ple_w.astype(BF16)
    gate_w16 = ple_gate_w.astype(BF16)
    ln_g = post_ln_g.reshape(DEPTH, 1, d)
    ln_b = post_ln_b.reshape(DEPTH, 1, d)
    for i in range(DEPTH):
        j = i // 2
        if i % 2 == 0:
            y_parts = list(_even_layer(x16, b, t, even_w, cmp_w, hg_norm, hgrn_lb, j))
            w_out = ev_w_out16
        else:
            y_parts = [_odd_layer(x16, b, t, od_w_in16, conv_w, conv_b, conv_g, conv_beta, j)]
            w_out = od_w_out16
        x32, x16 = _post(y_parts, w_out, j, x32, p_all, i, ple_w16, gate_w16, ln_g, ln_b, 256)
    return x32.reshape(b, t, d)
```

```python
import functools

import numpy as np
import jax
import jax.numpy as jnp
from jax import lax
from jax.experimental import pallas as pl
from jax.experimental.pallas import tpu as pltpu

F32 = jnp.float32
BF16 = jnp.bfloat16

D_MODEL = 2048
DEPTH = 4
PLE_DIM = 256
HEAD_DIM = 128
NSA_HEADS = 8
NSA_KV_GROUPS = 2
NSA_HPG = NSA_HEADS // NSA_KV_GROUPS
NSA_WIDTH = NSA_HEADS * HEAD_DIM
KV_W = NSA_KV_GROUPS * HEAD_DIM
CMP_LEN = 32
CMP_STRIDE = 16
SEL_BLOCK = 64
SEL_TOPK = 16
WINDOW = 512
Q_BLOCK = 128
HG_HEADS = 8
HG_DK = 128
HG_DV = 128
HG_WIDTH = HG_HEADS * HG_DV
CONV_CH = D_MODEL
CONV_K = 31
N_EVEN = (DEPTH + 1) // 2
DEEPNORM_ALPHA = (2.0 * DEPTH) ** 0.25
LN_EPS = 1e-5
LOG2E = 1.4426950408889634

NEG = -1e30
SEL_KEY_TILE = 2048
NSA_BATCH_PER_STEP = 1
HG_CHUNK = 128
HG_SUB = 4
HG_HEADS_PER_STEP = 8
CONV_ROWS = 128
CONV_HALO = 32
CONV_STRIP = 256
SUBLANES = 8
VMEM_LIMIT = 56 * 1024 * 1024


def _params(*sem):
    return pltpu.CompilerParams(dimension_semantics=sem, vmem_limit_bytes=VMEM_LIMIT)


def _dot(a, b):
    return jnp.dot(a, b, preferred_element_type=F32)


def _dot_nt(a, b):
    return lax.dot_general(a, b, (((1,), (1,)), ((), ())), preferred_element_type=F32)


def _dot_tn(a, b):
    return lax.dot_general(a, b, (((0,), (0,)), ((), ())), preferred_element_type=F32)


def _mm_body(x_ref, w_ref, o_ref):
    o_ref[...] = _dot(x_ref[...].astype(BF16), w_ref[...]).astype(o_ref.dtype)


def _mm_cast_body(x_ref, w_ref, o_ref, x16_ref):
    xb = x_ref[...].astype(BF16)
    o_ref[...] = _dot(xb, w_ref[...]).astype(o_ref.dtype)

    @pl.when(pl.program_id(1) == 0)
    def _():
        x16_ref[...] = xb


def _matmul(x, w, layer, out_dtype, bm, bn, name, emit_x16=False):
    m, k = x.shape
    n = w.shape[2]
    out_specs = [pl.BlockSpec((bm, bn), lambda i, j: (i, j))]
    out_shape = [jax.ShapeDtypeStruct((m, n), out_dtype)]
    if emit_x16:
        out_specs.append(pl.BlockSpec((bm, k), lambda i, j: (i, 0)))
        out_shape.append(jax.ShapeDtypeStruct((m, k), BF16))
    out = pl.pallas_call(
        _mm_cast_body if emit_x16 else _mm_body,
        grid=(m // bm, n // bn),
        in_specs=[pl.BlockSpec((bm, k), lambda i, j: (i, 0)),
                  pl.BlockSpec((None, k, bn), lambda i, j: (layer, 0, j))],
        out_specs=out_specs,
        out_shape=out_shape,
        compiler_params=_params("parallel", "arbitrary"),
        name=name,
    )(x, w)
    return out if emit_x16 else out[0]


def _cmp_body(r_ref, pe_ref, w1_ref, w2_ref, o_ref, *, n):
    row = lax.broadcasted_iota(jnp.int32, (n, 1), 0)
    for kv in range(2):
        pe8 = jnp.broadcast_to(pe_ref[kv], (8, CMP_LEN * HEAD_DIM)).astype(BF16)
        pe_term = _dot(pe8, w1_ref[kv])[0:1, :]
        for g in range(NSA_KV_GROUPS):
            first = jnp.zeros((n, HEAD_DIM), F32)
            second = jnp.zeros((n, HEAD_DIM), F32)
            for r in range(CMP_STRIDE):
                col = r * 2 * KV_W + kv * KV_W + g * HEAD_DIM
                xr = r_ref[:, col:col + HEAD_DIM]
                first = first + _dot(xr, w1_ref[kv, r * HEAD_DIM:(r + 1) * HEAD_DIM, :])
                lo = (CMP_STRIDE + r) * HEAD_DIM
                second = second + _dot(xr, w1_ref[kv, lo:lo + HEAD_DIM, :])
            pre = first + pltpu.roll(second, n - 1, 0) + pe_term
            out = _dot(jax.nn.gelu(pre).astype(BF16), w2_ref[kv])
            o_ref[kv, g] = jnp.where(row < n - 1, out, 0.0).astype(o_ref.dtype)


def _mm_rowgroup_body(x_ref, w_ref, o_ref, acc_ref):
    n = w_ref.shape[1]
    acc = _dot(x_ref[...].astype(BF16), w_ref[...])
    for c in range(n // HEAD_DIM):
        acc_ref[c] = acc[:, c * HEAD_DIM:(c + 1) * HEAD_DIM]
    groups = acc_ref.shape[1] // CMP_STRIDE
    for r in range(CMP_STRIDE):
        for c in range(n // HEAD_DIM):
            lo = r * n + c * HEAD_DIM
            o_ref[:, lo:lo + HEAD_DIM] = acc_ref[c, pl.ds(r, groups, stride=CMP_STRIDE), :].astype(o_ref.dtype)


def _matmul_rowgroup(x, w, layer, bm, name):
    m, k = x.shape
    n = w.shape[2]
    return pl.pallas_call(
        _mm_rowgroup_body,
        grid=(m // bm,),
        in_specs=[pl.BlockSpec((bm, k), lambda i: (i, 0)),
                  pl.BlockSpec((None, k, n), lambda i: (layer, 0, 0))],
        out_specs=pl.BlockSpec((bm // CMP_STRIDE, CMP_STRIDE * n), lambda i: (i, 0)),
        out_shape=jax.ShapeDtypeStruct((m // CMP_STRIDE, CMP_STRIDE * n), BF16),
        scratch_shapes=[pltpu.VMEM((n // HEAD_DIM, bm, HEAD_DIM), F32)],
        compiler_params=_params("parallel"),
        name=name,
    )(x, w)


def _compress(r, pe, w1, w2, layer):
    b, n, _ = r.shape
    return pl.pallas_call(
        functools.partial(_cmp_body, n=n),
        grid=(b,),
        in_specs=[pl.BlockSpec((None, n, CMP_STRIDE * 2 * KV_W), lambda i: (i, 0, 0)),
                  pl.BlockSpec((None, 2, 1, CMP_LEN * HEAD_DIM), lambda i: (layer, 0, 0, 0)),
                  pl.BlockSpec((None, 2, CMP_LEN * HEAD_DIM, HEAD_DIM), lambda i: (layer, 0, 0, 0)),
                  pl.BlockSpec((None, 2, HEAD_DIM, HEAD_DIM), lambda i: (layer, 0, 0, 0))],
        out_specs=pl.BlockSpec((None, 2, NSA_KV_GROUPS, n, HEAD_DIM), lambda i: (i, 0, 0, 0, 0)),
        out_shape=jax.ShapeDtypeStruct((b, 2, NSA_KV_GROUPS, n, HEAD_DIM), BF16),
        compiler_params=_params("parallel"),
        name="nsa_compress",
    )(r, pe, w1, w2)


def _masked_softmax(s, mask):
    m = jnp.max(s, axis=-1, keepdims=True)
    e = jnp.where(mask, jnp.exp(s - m), 0.0)
    l = jnp.sum(e, axis=-1, keepdims=True)
    return e / jnp.where(l > 0.0, l, 1.0)


def _nsa_body(q_ref, ks_ref, vs_ref, *rest, t_len, nb):
    nwin = WINDOW // Q_BLOCK + 1
    kw_refs, vw_refs = rest[:nwin], rest[nwin:2 * nwin]
    kc_ref, vc_ref, gate_ref, nz_ref, ov_ref, eh_ref, o_ref, m_ref, l_ref, acc_ref = rest[2 * nwin:]
    i = pl.program_id(1)
    t0 = i * Q_BLOCK
    rows = NSA_HPG * Q_BLOCK
    n_cmp = t_len // CMP_STRIDE
    n_sel = t_len // SEL_BLOCK
    kt = SEL_KEY_TILE
    streams = [(bb, g) for bb in range(nb) for g in range(NSA_KV_GROUPS)]

    def lanes(g, width):
        return slice(g * width, (g + 1) * width)

    rowi = lax.broadcasted_iota(jnp.int32, (rows, 1), 0)
    tcol = t0 + jnp.bitwise_and(rowi, Q_BLOCK - 1)
    head_slope = lambda g, h: 2.0 ** -(g * NSA_HPG + h + 1)
    slopes = [jnp.concatenate([jnp.full((Q_BLOCK, 1), head_slope(g, h), F32) for h in range(NSA_HPG)],
                              axis=0) for g in range(NSA_KV_GROUPS)]
    q4 = [jnp.concatenate([q_ref[bb, :, g * rows + h * HEAD_DIM:g * rows + (h + 1) * HEAD_DIM]
                           for h in range(NSA_HPG)], axis=0) for bb, g in streams]

    nidx = lax.broadcasted_iota(jnp.int32, (1, n_cmp), 1)
    d_c = tcol - (nidx * CMP_STRIDE + CMP_LEN - 1)
    mask_c = d_c >= 0
    d_cf = d_c.astype(F32)
    qk_c = [_dot_nt(q4[si], kc_ref[bb, g]) for si, (bb, g) in enumerate(streams)]
    qk_w = [_dot_nt(q4[si], jnp.concatenate([r[bb, :, lanes(g, HEAD_DIM)] for r in kw_refs], axis=0))
            for si, (bb, g) in enumerate(streams)]
    o_c, imp = [], []
    for si, (bb, g) in enumerate(streams):
        s = jnp.where(mask_c, qk_c[si] - slopes[g] * d_cf, NEG)
        p_c = _masked_softmax(s, mask_c)
        o_c.append(_dot(p_c.astype(BF16), vc_ref[bb, g]))
        p_sum = p_c[0:Q_BLOCK]
        for h in range(1, NSA_HPG):
            p_sum = p_sum + p_c[h * Q_BLOCK:(h + 1) * Q_BLOCK]
        p_hi = p_sum.astype(BF16)
        p_lo = (p_sum - p_hi.astype(F32)).astype(BF16)
        imp.append(_dot(p_hi, ov_ref[...]) + _dot(p_lo, ov_ref[...]))

    span = WINDOW + Q_BLOCK
    kpos_w = (t0 - WINDOW) + lax.broadcasted_iota(jnp.int32, (1, span), 1)
    d_w = tcol - kpos_w
    mask_w = (d_w >= 0) & (d_w < WINDOW) & (kpos_w >= 0)
    d_wf = d_w.astype(F32)
    o_w = []
    for si, (bb, g) in enumerate(streams):
        vw = jnp.concatenate([r[bb, :, lanes(g, HEAD_DIM)] for r in vw_refs], axis=0)
        s = jnp.where(mask_w, qk_w[si] - slopes[g] * d_wf, NEG)
        p_w = _masked_softmax(s, mask_w)
        o_w.append(_dot(p_w.astype(BF16), vw))

    jrow = lax.broadcasted_iota(jnp.int32, (n_sel, 1), 0)
    jf = jrow.astype(F32)
    tt = t0 + lax.broadcasted_iota(jnp.int32, (1, Q_BLOCK), 1)
    cur = tt // SEL_BLOCK
    forced = (jrow == 0) | (jrow == cur) | (jrow == cur - 1)
    future = jrow * SEL_BLOCK > tt
    work = [jnp.where(future, -1e9, jnp.where(forced, 1e9, im.T)) for im in imp]
    picked = [jnp.zeros((n_sel, Q_BLOCK), F32) for _ in streams]
    for _ in range(min(SEL_TOPK, n_sel)):
        for si in range(len(streams)):
            mx = jnp.max(work[si], axis=0, keepdims=True)
            first = jnp.min(jnp.where(work[si] == mx, jf, 1e9), axis=0, keepdims=True)
            pick = jf == first
            picked[si] = jnp.where(pick, 1.0, picked[si])
            work[si] = jnp.where(pick, -3e38, work[si])
    sel = [jnp.where(future, 0.0, pk).T > 0.5 for pk in picked]

    jidx = lax.broadcasted_iota(jnp.int32, (1, n_sel), 1)
    blk_off = jidx.astype(F32) * float(SEL_BLOCK)
    q_aug = []
    for si, (bb, g) in enumerate(streams):
        parts = []
        for h in range(NSA_HPG):
            bias_h = jnp.where(jidx == 0, head_slope(g, h),
                               jnp.where(sel[si], head_slope(g, h) * blk_off, NEG)).astype(BF16)
            parts.append(jnp.concatenate([q4[si][h * Q_BLOCK:(h + 1) * Q_BLOCK], bias_h], axis=1))
        q_aug.append(jnp.concatenate(parts, axis=0))
        m_ref[si] = jnp.full((rows, 1), NEG, F32)
        l_ref[si] = jnp.zeros((rows, 1), F32)
        acc_ref[si] = jnp.zeros((rows, HEAD_DIM), F32)

    def sel_tile(c, causal):
        k0 = pl.multiple_of(c * kt, kt)
        eh = eh_ref[pl.ds(k0, kt), :]
        scores = []
        for si, (bb, g) in enumerate(streams):
            k_aug = jnp.concatenate([ks_ref[bb, pl.ds(k0, kt), lanes(g, HEAD_DIM)], eh], axis=1)
            scores.append(_dot_nt(q_aug[si], k_aug))
        for si, (bb, g) in enumerate(streams):
            sc = scores[si]
            if causal:
                kpos = k0 + lax.broadcasted_iota(jnp.int32, (1, kt), 1)
                sc = jnp.where(kpos <= tcol, sc, NEG)
            m_old = m_ref[si]
            m_new = jnp.maximum(m_old, jnp.max(sc, axis=-1, keepdims=True))
            alpha = jnp.exp(m_old - m_new)
            p = jnp.exp(sc - m_new)
            l_ref[si] = alpha * l_ref[si] + jnp.sum(p, axis=-1, keepdims=True)
            acc_ref[si] = alpha * acc_ref[si] + _dot(p.astype(BF16),
                                                     vs_ref[bb, pl.ds(k0, kt), lanes(g, HEAD_DIM)])
            m_ref[si] = m_new

    c_last = t0 // kt

    def past_tile(c, carry):
        sel_tile(c, False)
        return carry

    lax.fori_loop(0, c_last, past_tile, 0)
    sel_tile(c_last, True)

    for si, (bb, g) in enumerate(streams):
        o_s = acc_ref[si] / l_ref[si]
        gt = jax.nn.sigmoid(gate_ref[bb, :, lanes(g, HEAD_DIM)])
        outs = []
        for h in range(NSA_HPG):
            r0, r1 = h * Q_BLOCK, (h + 1) * Q_BLOCK
            a = (gt[:, 3 * h:3 * h + 1] * o_c[si][r0:r1] + gt[:, 3 * h + 1:3 * h + 2] * o_s[r0:r1]
                 + gt[:, 3 * h + 2:3 * h + 3] * o_w[si][r0:r1])
            c0 = g * rows + h * HEAD_DIM
            outs.append(a * jax.nn.silu(nz_ref[bb, :, c0:c0 + HEAD_DIM]))
        o_ref[bb, :, lanes(g, rows)] = jnp.concatenate(outs, axis=1).astype(o_ref.dtype)


def _nsa_constants(t_len):
    n_cmp = t_len // CMP_STRIDE
    n_sel = t_len // SEL_BLOCK
    s = np.arange(n_cmp)[:, None] * CMP_STRIDE
    b = np.arange(n_sel)[None, :] * SEL_BLOCK
    ov = np.clip(np.minimum(s + CMP_LEN, b + SEL_BLOCK) - np.maximum(s, b), 0, None) / CMP_LEN
    ov[n_cmp - 1] = 0.0
    key_cols = (np.arange(t_len)[:, None] // SEL_BLOCK == np.arange(n_sel)[None, :]).astype(np.float32)
    key_cols[:, 0] = np.arange(t_len) % SEL_BLOCK
    return jnp.asarray(ov, BF16), jnp.asarray(key_cols, BF16)


def _nsa(p16, cmp_kv, gates, p32):
    b, t, _ = p16.shape
    nb = NSA_BATCH_PER_STEP
    n_cmp = t // CMP_STRIDE
    n_sel = t // SEL_BLOCK
    rows = NSA_HPG * Q_BLOCK
    nstreams = nb * NSA_KV_GROUPS
    ov, key_cols = _nsa_constants(t)
    kv_base = NSA_WIDTH // KV_W
    once = pl.Buffered(1)

    def kv_spec(section):
        return pl.BlockSpec((nb, t, KV_W), lambda bi, i: (bi, 0, kv_base + section), pipeline_mode=once)

    nwin = WINDOW // Q_BLOCK + 1

    def win_specs(section):
        return [pl.BlockSpec((nb, Q_BLOCK, KV_W),
                             lambda bi, i, k=k: (bi, jnp.maximum(i - (nwin - 1) + k, 0), kv_base + section))
                for k in range(nwin)]

    def cmp_spec(kv):
        return pl.BlockSpec((nb, None, NSA_KV_GROUPS, n_cmp, HEAD_DIM), lambda bi, i: (bi, kv, 0, 0, 0),
                            pipeline_mode=once)

    return pl.pallas_call(
        functools.partial(_nsa_body, t_len=t, nb=nb),
        grid=(b // nb, t // Q_BLOCK),
        in_specs=[
            pl.BlockSpec((nb, Q_BLOCK, NSA_WIDTH), lambda bi, i: (bi, i, 0)),
            kv_spec(0), kv_spec(1), *win_specs(2), *win_specs(3),
            cmp_spec(0), cmp_spec(1),
            pl.BlockSpec((nb, Q_BLOCK, NSA_KV_GROUPS * HEAD_DIM), lambda bi, i: (bi, i, 0)),
            pl.BlockSpec((nb, Q_BLOCK, NSA_WIDTH), lambda bi, i: (bi, i, 0)),
            pl.BlockSpec((n_cmp, n_sel), lambda bi, i: (0, 0), pipeline_mode=once),
            pl.BlockSpec((t, n_sel), lambda bi, i: (0, 0), pipeline_mode=once),
        ],
        out_specs=pl.BlockSpec((nb, Q_BLOCK, NSA_WIDTH), lambda bi, i: (bi, i, 0)),
        out_shape=jax.ShapeDtypeStruct((b, t, NSA_WIDTH), BF16),
        scratch_shapes=[pltpu.VMEM((nstreams, rows, 1), F32), pltpu.VMEM((nstreams, rows, 1), F32),
                        pltpu.VMEM((nstreams, rows, HEAD_DIM), F32)],
        compiler_params=_params("parallel", "arbitrary"),
        name="nsa_attention",
    )(*([p16] * (3 + 2 * nwin)), cmp_kv, cmp_kv, gates, p32, ov, key_cols)


def _hgrn_level_masks():
    t = np.arange(HG_CHUNK)[:, None]
    s = np.arange(HG_CHUNK)[None, :]
    masks = []
    w = HG_SUB
    while w < HG_CHUNK:
        masks.append((t // (2 * w) == s // (2 * w)) & (t % (2 * w) >= w) & (s % (2 * w) < w))
        w *= 2
    return jnp.asarray(np.stack(masks), F32)


def _hgrn_body(hq_ref, hf_ref, hi_ref, hz_ref, lb_ref, nw_ref, mk_ref, o_ref, st_ref, *, layer):
    c = HG_CHUNK

    @pl.when(pl.program_id(2) == 0)
    def _():
        st_ref[...] = jnp.zeros_like(st_ref)

    ri = lax.broadcasted_iota(jnp.int32, (c, 1), 0)
    ci = lax.broadcasted_iota(jnp.int32, (1, c), 1)
    tri = jnp.where(ci <= ri, 1.0, 0.0).astype(BF16)
    in_sub = jnp.bitwise_and(ri, HG_SUB - 1)

    heads = range(HG_HEADS_PER_STEP)
    lane = lambda hh: slice(hh * HG_DK, (hh + 1) * HG_DK)

    q, k, v, vb, g = [], [], [], [], []
    for hh in heads:
        ln = lane(hh)
        lbraw = lb_ref[:, ln]
        ex = jnp.exp(lbraw - jnp.max(lbraw, axis=0, keepdims=True))
        sm = ex / jnp.sum(ex, axis=0, keepdims=True)
        lb = jnp.zeros((1, HG_DK), F32)
        for r in range(1, layer + 1):
            lb = lb + sm[r:r + 1]
        hf = hf_ref[:, ln]
        q.append(jax.nn.silu(hq_ref[:, ln]))
        v.append(hi_ref[:, ln])
        vb.append(v[hh].astype(BF16))
        sg = jax.nn.sigmoid(hf)
        f = lb + (1.0 - lb) * sg
        k.append((1.0 - lb) * (1.0 - sg))
        lf = jnp.log(f) * LOG2E
        lf1 = lf.astype(BF16)
        rem = lf - lf1.astype(F32)
        lf2 = rem.astype(BF16)
        lf3 = (rem - lf2.astype(F32)).astype(BF16)
        g.append(_dot(tri, lf1) + _dot(tri, lf2) + _dot(tri, lf3))

    st = [st_ref[hh] for hh in heads]
    o = [_dot_nt((q[hh] * jnp.exp2(g[hh])).astype(BF16), st[hh].astype(BF16)) for hh in heads]

    a_off = [jnp.zeros((c, c), F32) for _ in heads]
    w = HG_SUB
    level = 0
    while w < c:
        for hh in heads:
            gref = jnp.concatenate(
                [jnp.broadcast_to(g[hh][base + w - 1:base + w], (2 * w, HG_DK)) for base in range(0, c, 2 * w)],
                axis=0)
            e = jnp.exp2(-jnp.abs(g[hh] - gref))
            a_off[hh] = a_off[hh] + mk_ref[level] * _dot_nt((q[hh] * e).astype(BF16), (k[hh] * e).astype(BF16))
        w *= 2
        level += 1
    for hh in heads:
        o[hh] = o[hh] + _dot(a_off[hh].astype(BF16), vb[hh])

    for hh in heads:
        ln = lane(hh)
        oh = o[hh] + jnp.sum(q[hh] * k[hh], axis=-1, keepdims=True) * v[hh]
        for d in range(1, HG_SUB):
            live = in_sub >= d
            dec = jnp.exp2(jnp.where(live, g[hh] - pltpu.roll(g[hh], d, 0), 0.0))
            a_col = jnp.sum(q[hh] * (pltpu.roll(k[hh], d, 0) * dec), axis=-1, keepdims=True)
            oh = oh + jnp.where(live, a_col, 0.0) * pltpu.roll(v[hh], d, 0)
        oh = oh * lax.rsqrt(jnp.mean(oh * oh, axis=-1, keepdims=True) + LN_EPS)
        o_ref[:, ln] = (oh * nw_ref[:, ln] * jax.nn.silu(hz_ref[:, ln])).astype(o_ref.dtype)

        g_last = g[hh][c - 1:c]
        kd = (k[hh] * jnp.exp2(g_last - g[hh])).astype(BF16)
        st_ref[hh] = st[hh] * jnp.exp2(g_last) + _dot_tn(vb[hh], kd)


def _hgrn(p32, lb_raw, norm_w, layer):
    b, t, _ = p32.shape
    hps = HG_HEADS_PER_STEP
    wide = hps * HG_DK
    base = NSA_WIDTH // wide
    per_section = HG_WIDTH // wide
    masks = _hgrn_level_masks()

    def col_spec(section):
        return pl.BlockSpec((None, HG_CHUNK, wide), lambda bi, h, c: (bi, c, base + section * per_section + h))

    return pl.pallas_call(
        functools.partial(_hgrn_body, layer=layer),
        grid=(b, HG_HEADS // hps, t // HG_CHUNK),
        in_specs=[col_spec(0), col_spec(1), col_spec(2), col_spec(3),
                  pl.BlockSpec((N_EVEN, wide), lambda bi, h, c: (0, h)),
                  pl.BlockSpec((None, 1, wide), lambda bi, h, c: (layer, 0, h)),
                  pl.BlockSpec(masks.shape, lambda bi, h, c: (0, 0, 0))],
        out_specs=pl.BlockSpec((None, HG_CHUNK, wide), lambda bi, h, c: (bi, c, h)),
        out_shape=jax.ShapeDtypeStruct((b, t, HG_WIDTH), BF16),
        scratch_shapes=[pltpu.VMEM((hps, HG_DV, HG_DK), F32)],
        compiler_params=_params("parallel", "parallel", "arbitrary"),
        name="hgrn2",
    )(p32, p32, p32, p32, lb_raw, norm_w, masks)


def _glu_body(x_ref, wa_ref, wb_ref, wz_ref, u_ref, sz_ref):
    x = x_ref[...].astype(BF16)
    u_ref[...] = _dot(x, wa_ref[...]) * jax.nn.sigmoid(_dot(x, wb_ref[...]))
    sz_ref[...] = jax.nn.silu(_dot(x, wz_ref[...]))


def _glu_proj(x, w, layer, bm, bn):
    m, k = x.shape
    nb = CONV_CH // bn

    def w_spec(part):
        return pl.BlockSpec((None, k, bn), lambda i, j: (layer, 0, part * nb + j))

    out = jax.ShapeDtypeStruct((m, CONV_CH), F32)
    return pl.pallas_call(
        _glu_body,
        grid=(m // bm, nb),
        in_specs=[pl.BlockSpec((bm, k), lambda i, j: (i, 0)), w_spec(0), w_spec(1), w_spec(2)],
        out_specs=[pl.BlockSpec((bm, bn), lambda i, j: (i, j))] * 2,
        out_shape=[out, out],
        compiler_params=_params("parallel", "arbitrary"),
        name="conv_glu_proj",
    )(x, w, w, w)


def _conv_body(uc_ref, up_ref, sz_ref, w_ref, cb_ref, g_ref, b_ref, o_ref, buf_ref, c_ref):
    first_tile = pl.program_id(1) == 0
    buf_ref[0:CONV_HALO, :] = jnp.where(first_tile, 0.0, up_ref[...])
    buf_ref[CONV_HALO:CONV_HALO + CONV_ROWS, :] = uc_ref[...]
    lead = CONV_HALO - (CONV_K - 1)
    buf_rows = CONV_HALO + CONV_ROWS
    for cs in range(0, CONV_CH, CONV_STRIP):
        x = buf_ref[:, cs:cs + CONV_STRIP]
        acc = jnp.broadcast_to(cb_ref[:, cs:cs + CONV_STRIP], (CONV_ROWS, CONV_STRIP))
        for phase in range(SUBLANES):
            xr = x if phase == 0 else pltpu.roll(x, buf_rows - phase, 0)
            for k in range(CONV_K):
                if (lead + k) % SUBLANES == phase:
                    a = lead + k - phase
                    acc = acc + xr[a:a + CONV_ROWS] * w_ref[k:k + 1, cs:cs + CONV_STRIP]
        c_ref[:, cs:cs + CONV_STRIP] = acc
    c = c_ref[...]
    mu = jnp.mean(c, axis=-1, keepdims=True)
    d = c - mu
    var = jnp.mean(d * d, axis=-1, keepdims=True)
    cn = d * lax.rsqrt(var + LN_EPS) * g_ref[...] + b_ref[...]
    o_ref[...] = (jax.nn.silu(cn) * sz_ref[...]).astype(o_ref.dtype)


def _conv_module(u, sz, conv_w, conv_b, ln_g, ln_b, layer):
    b, t, ch = u.shape
    halo_per_tile = CONV_ROWS // CONV_HALO
    row = lambda: pl.BlockSpec((None, 1, ch), lambda bi, i: (layer, 0, 0))
    tile = lambda: pl.BlockSpec((None, CONV_ROWS, ch), lambda bi, i: (bi, i, 0))
    return pl.pallas_call(
        _conv_body,
        grid=(b, t // CONV_ROWS),
        in_specs=[tile(),
                  pl.BlockSpec((None, CONV_HALO, ch),
                               lambda bi, i: (bi, jnp.maximum(i * halo_per_tile - 1, 0), 0)),
                  tile(),
                  pl.BlockSpec((None, CONV_HALO, ch), lambda bi, i: (layer, 0, 0)),
                  row(), row(), row()],
        out_specs=tile(),
        out_shape=jax.ShapeDtypeStruct((b, t, ch), BF16),
        scratch_shapes=[pltpu.VMEM((CONV_HALO + CONV_ROWS, ch), F32), pltpu.VMEM((CONV_ROWS, ch), F32)],
        compiler_params=_params("parallel", "arbitrary"),
        name="conv_module",
    )(u, u, sz, conv_w, conv_b, ln_g, ln_b)


def _post_body(*refs, nparts):
    y_refs = refs[:nparts]
    w_refs = refs[nparts:2 * nparts]
    x_ref, p_ref, plw_ref, gw_ref, lg_ref, lb_ref, o32_ref, o16_ref = refs[2 * nparts:]
    y = _dot(y_refs[0][...], w_refs[0][...])
    for r in range(1, nparts):
        y = y + _dot(y_refs[r][...], w_refs[r][...])
    ple = _dot(p_ref[...].astype(BF16), plw_ref[...])
    h = DEEPNORM_ALPHA * x_ref[...] + y
    mu = jnp.mean(h, axis=-1, keepdims=True)
    d = h - mu
    var = jnp.mean(d * d, axis=-1, keepdims=True)
    x1 = d * lax.rsqrt(var + LN_EPS) * lg_ref[...] + lb_ref[...]
    gate = jax.nn.sigmoid(_dot(x1.astype(BF16), gw_ref[...]))
    out = x1 + ple * gate
    o32_ref[...] = out
    o16_ref[...] = out.astype(BF16)


def _post(y_parts, w_out, w_layer, x, p, layer, ple_w, gate_w, ln_g, ln_b, bm):
    m, d = x.shape
    nparts = len(y_parts)
    kp = y_parts[0].shape[1]
    once = pl.Buffered(1)
    in_specs = ([pl.BlockSpec((bm, kp), lambda i: (i, 0)) for _ in y_parts]
                + [pl.BlockSpec((None, kp, d), lambda i, r=r: (w_layer, r, 0), pipeline_mode=once)
                   for r in range(nparts)]
                + [pl.BlockSpec((bm, d), lambda i: (i, 0)),
                   pl.BlockSpec((None, bm, PLE_DIM), lambda i: (layer, i, 0)),
                   pl.BlockSpec((None, PLE_DIM, d), lambda i: (layer, 0, 0), pipeline_mode=once),
                   pl.BlockSpec((None, d, d), lambda i: (layer, 0, 0), pipeline_mode=once),
                   pl.BlockSpec((None, 1, d), lambda i: (layer, 0, 0), pipeline_mode=once),
                   pl.BlockSpec((None, 1, d), lambda i: (layer, 0, 0), pipeline_mode=once)])
    return pl.pallas_call(
        functools.partial(_post_body, nparts=nparts),
        grid=(m // bm,),
        in_specs=in_specs,
        out_specs=[pl.BlockSpec((bm, d), lambda i: (i, 0))] * 2,
        out_shape=[jax.ShapeDtypeStruct((m, d), F32), jax.ShapeDtypeStruct((m, d), BF16)],
        compiler_params=_params("parallel"),
        name="outproj_deepnorm_ple",
    )(*y_parts, *([w_out] * nparts), x, p, ple_w, gate_w, ln_g, ln_b)


EVEN_SECTIONS = np.cumsum([0, NSA_WIDTH, KV_W, KV_W, KV_W, KV_W, KV_W, KV_W, NSA_HEADS * 3, NSA_WIDTH,
                           HG_WIDTH, HG_WIDTH, HG_WIDTH, HG_WIDTH])
WPREP_COLS = 256
WPREP_PIECE = 128


def _even_weights_body(wt_ref, w16_ref, wc_ref, w32_ref, wg_ref):
    o = EVEN_SECTIONS

    def move(dst_ref, dst_lo, src_lo, src_hi, scale=None):
        for r in range(src_lo, src_hi, WPREP_PIECE):
            piece = wt_ref[r:r + WPREP_PIECE, :]
            if scale is not None:
                piece = piece * scale
            c = dst_lo + r - src_lo
            dst_ref[:, c:c + WPREP_PIECE] = piece.T.astype(BF16)

    move(w16_ref, 0, o[0], o[1], HEAD_DIM ** -0.5)
    move(w16_ref, NSA_WIDTH, o[3], o[7])
    move(wc_ref, 0, o[1], o[3])
    move(w32_ref, 0, o[8], o[13])
    per_group = NSA_HPG * 3
    gt = wt_ref[o[7]:o[7] + WPREP_PIECE, :].T
    lane = lax.broadcasted_iota(jnp.int32, gt.shape, 1)
    for g in range(NSA_KV_GROUPS):
        shifted = gt if g == 0 else pltpu.roll(gt, WPREP_PIECE - g * per_group, 1)
        wg_ref[:, g * HEAD_DIM:(g + 1) * HEAD_DIM] = jnp.where(lane < per_group, shifted, 0.0).astype(BF16)


def _even_weights(w_in):
    nl, k, n = w_in.shape
    widths = (NSA_WIDTH + 4 * KV_W, 2 * KV_W, NSA_WIDTH + 4 * HG_WIDTH, NSA_KV_GROUPS * HEAD_DIM)
    return pl.pallas_call(
        _even_weights_body,
        grid=(nl, k // WPREP_COLS),
        in_specs=[pl.BlockSpec((None, n, WPREP_COLS), lambda l, i: (l, 0, i))],
        out_specs=[pl.BlockSpec((None, WPREP_COLS, wd), lambda l, i: (l, i, 0)) for wd in widths],
        out_shape=[jax.ShapeDtypeStruct((nl, k, wd), BF16) for wd in widths],
        compiler_params=_params("parallel", "parallel"),
        name="even_weight_prep",
    )(jnp.swapaxes(w_in, 1, 2))


def _even_layer(x16, b, t, weights, cmp_weights, hg_norm, lb_raw, layer):
    w16, wc, w32, wg = weights
    if x16.dtype == BF16:
        p16 = _matmul(x16, w16, layer, BF16, 1024, 1024, "even_proj_qkv")
    else:
        p16, x16 = _matmul(x16, w16, layer, BF16, 1024, 1024, "even_proj_qkv_cast", emit_x16=True)
    pc = _matmul_rowgroup(x16, wc, layer, 1024, "even_proj_cmp")
    p32 = _matmul(x16, w32, layer, F32, 1024, 1024, "even_proj_gates")
    gates = _matmul(x16, wg, layer, F32, 1024, NSA_KV_GROUPS * HEAD_DIM, "even_proj_nsa_gates")
    cmp_kv = _compress(pc.reshape(b, t // CMP_STRIDE, -1), *cmp_weights, layer)
    p32 = p32.reshape(b, t, -1)
    ya = _nsa(p16.reshape(b, t, -1), cmp_kv, gates.reshape(b, t, -1), p32)
    yo = _hgrn(p32, lb_raw, hg_norm, layer)
    return ya.reshape(b * t, NSA_WIDTH), yo.reshape(b * t, HG_WIDTH)


def _odd_layer(x16, b, t, w_in16, conv_w, conv_b, ln_g, ln_b, layer):
    u, sz = _glu_proj(x16, w_in16, layer, 1024, 512)
    y = _conv_module(u.reshape(b, t, CONV_CH), sz.reshape(b, t, CONV_CH), conv_w, conv_b, ln_g, ln_b, layer)
    return y.reshape(b * t, CONV_CH)


def kernel(x, p, ev_w_in, ev_cmp_pe_k, ev_cmp_w1_k, ev_cmp_w2_k, ev_cmp_pe_v, ev_cmp_w1_v,
           ev_cmp_w2_v, ev_hg_norm, hgrn_lb, ev_w_out, od_w_in, od_conv_w, od_conv_b, od_ln_g,
           od_ln_b, od_w_out, post_ln_g, post_ln_b, ple_w, ple_gate_w):
    b, t, d = x.shape
    x32 = x.reshape(b * t, d)
    x16 = x32
    even_w = _even_weights(ev_w_in)
    n_even = ev_w_in.shape[0]
    cmp_w = (jnp.stack([ev_cmp_pe_k, ev_cmp_pe_v], axis=1).reshape(n_even, 2, 1, CMP_LEN * HEAD_DIM),
             jnp.stack([ev_cmp_w1_k, ev_cmp_w1_v], axis=1).astype(BF16),
             jnp.stack([ev_cmp_w2_k, ev_cmp_w2_v], axis=1).astype(BF16))
    hg_norm = ev_hg_norm.reshape(n_even, 1, HG_WIDTH)
    ev_w_out16 = ev_w_out.astype(BF16)
    od_w_in16 = od_w_in.astype(BF16)
    od_w_out16 = od_w_out.astype(BF16)
    n_odd = od_w_in.shape[0]
    conv_w = jnp.pad(od_conv_w, ((0, 0), (0, CONV_HALO - CONV_K), (0, 0)))
    conv_b = od_conv_b.reshape(n_odd, 1, CONV_CH)
    conv_g = od_ln_g.reshape(n_odd, 1, CONV_CH)
    conv_beta = od_ln_b.reshape(n_odd, 1, CONV_CH)
    p_all = p.reshape(DEPTH, b * t, PLE_DIM)
    ple_w16 = ple_w.astype(BF16)
    gate_w16 = ple_gate_w.astype(BF16)
    ln_g = post_ln_g.reshape(DEPTH, 1, d)
    ln_b = post_ln_b.reshape(DEPTH, 1, d)
    for i in range(DEPTH):
        j = i // 2
        if i % 2 == 0:
            y_parts = list(_even_layer(x16, b, t, even_w, cmp_w, hg_norm, hgrn_lb, j))
            w_out = ev_w_out16
        else:
            y_parts = [_odd_layer(x16, b, t, od_w_in16, conv_w, conv_b, conv_g, conv_beta, j)]
            w_out = od_w_out16
        x32, x16 = _post(y_parts, w_out, j, x32, p_all, i, ple_w16, gate_w16, ln_g, ln_b, 256)
    return x32.reshape(b, t, d)
```

```python
import functools

import numpy as np
import jax
import jax.numpy as jnp
from jax import lax
from jax.experimental import pallas as pl
from jax.experimental.pallas import tpu as pltpu

F32 = jnp.float32
BF16 = jnp.bfloat16

D_MODEL = 2048
DEPTH = 4
PLE_DIM = 256
HEAD_DIM = 128
NSA_HEADS = 8
NSA_KV_GROUPS = 2
NSA_HPG = NSA_HEADS // NSA_KV_GROUPS
NSA_WIDTH = NSA_HEADS * HEAD_DIM
KV_W = NSA_KV_GROUPS * HEAD_DIM
CMP_LEN = 32
CMP_STRIDE = 16
SEL_BLOCK = 64
SEL_TOPK = 16
WINDOW = 512
Q_BLOCK = 128
HG_HEADS = 8
HG_DK = 128
HG_DV = 128
HG_WIDTH = HG_HEADS * HG_DV
CONV_CH = D_MODEL
CONV_K = 31
N_EVEN = (DEPTH + 1) // 2
DEEPNORM_ALPHA = (2.0 * DEPTH) ** 0.25
LN_EPS = 1e-5
LOG2E = 1.4426950408889634

NEG = -1e30
SEL_KEY_TILE = 2048
SEL_DIAG_WIDTHS = (SEL_KEY_TILE // 2, 3 * SEL_KEY_TILE // 4, SEL_KEY_TILE)
NSA_BATCH_PER_STEP = 1
HG_CHUNK = 128
HG_SUB = 4
HG_HEADS_PER_STEP = 8
CONV_ROWS = 128
CONV_HALO = 32
CONV_STRIP = 256
SUBLANES = 8
VMEM_LIMIT = 56 * 1024 * 1024


def _params(*sem):
    return pltpu.CompilerParams(dimension_semantics=sem, vmem_limit_bytes=VMEM_LIMIT)


def _dot(a, b):
    return jnp.dot(a, b, preferred_element_type=F32)


def _dot_nt(a, b):
    return lax.dot_general(a, b, (((1,), (1,)), ((), ())), preferred_element_type=F32)


def _dot_tn(a, b):
    return lax.dot_general(a, b, (((0,), (0,)), ((), ())), preferred_element_type=F32)


def _mm_body(x_ref, w_ref, o_ref):
    o_ref[...] = _dot(x_ref[...].astype(BF16), w_ref[...]).astype(o_ref.dtype)


def _mm_cast_body(x_ref, w_ref, o_ref, x16_ref):
    xb = x_ref[...].astype(BF16)
    o_ref[...] = _dot(xb, w_ref[...]).astype(o_ref.dtype)

    @pl.when(pl.program_id(1) == 0)
    def _():
        x16_ref[...] = xb


def _matmul(x, w, layer, out_dtype, bm, bn, name, emit_x16=False):
    m, k = x.shape
    n = w.shape[2]
    out_specs = [pl.BlockSpec((bm, bn), lambda i, j: (i, j))]
    out_shape = [jax.ShapeDtypeStruct((m, n), out_dtype)]
    if emit_x16:
        out_specs.append(pl.BlockSpec((bm, k), lambda i, j: (i, 0)))
        out_shape.append(jax.ShapeDtypeStruct((m, k), BF16))
    out = pl.pallas_call(
        _mm_cast_body if emit_x16 else _mm_body,
        grid=(m // bm, n // bn),
        in_specs=[pl.BlockSpec((bm, k), lambda i, j: (i, 0)),
                  pl.BlockSpec((None, k, bn), lambda i, j: (layer, 0, j))],
        out_specs=out_specs,
        out_shape=out_shape,
        compiler_params=_params("parallel", "arbitrary"),
        name=name,
    )(x, w)
    return out if emit_x16 else out[0]


def _cmp_body(r_ref, pe_ref, w1_ref, w2_ref, o_ref, *, n):
    row = lax.broadcasted_iota(jnp.int32, (n, 1), 0)
    for kv in range(2):
        pe8 = jnp.broadcast_to(pe_ref[kv], (8, CMP_LEN * HEAD_DIM)).astype(BF16)
        pe_term = _dot(pe8, w1_ref[kv])[0:1, :]
        for g in range(NSA_KV_GROUPS):
            first = jnp.zeros((n, HEAD_DIM), F32)
            second = jnp.zeros((n, HEAD_DIM), F32)
            for r in range(CMP_STRIDE):
                col = r * 2 * KV_W + kv * KV_W + g * HEAD_DIM
                xr = r_ref[:, col:col + HEAD_DIM]
                first = first + _dot(xr, w1_ref[kv, r * HEAD_DIM:(r + 1) * HEAD_DIM, :])
                lo = (CMP_STRIDE + r) * HEAD_DIM
                second = second + _dot(xr, w1_ref[kv, lo:lo + HEAD_DIM, :])
            pre = first + pltpu.roll(second, n - 1, 0) + pe_term
            out = _dot(jax.nn.gelu(pre).astype(BF16), w2_ref[kv])
            o_ref[kv, g] = jnp.where(row < n - 1, out, 0.0).astype(o_ref.dtype)


def _mm_rowgroup_body(x_ref, w_ref, o_ref, acc_ref):
    n = w_ref.shape[1]
    acc = _dot(x_ref[...].astype(BF16), w_ref[...])
    for c in range(n // HEAD_DIM):
        acc_ref[c] = acc[:, c * HEAD_DIM:(c + 1) * HEAD_DIM]
    groups = acc_ref.shape[1] // CMP_STRIDE
    for r in range(CMP_STRIDE):
        for c in range(n // HEAD_DIM):
            lo = r * n + c * HEAD_DIM
            o_ref[:, lo:lo + HEAD_DIM] = acc_ref[c, pl.ds(r, groups, stride=CMP_STRIDE), :].astype(o_ref.dtype)


def _matmul_rowgroup(x, w, layer, bm, name):
    m, k = x.shape
    n = w.shape[2]
    return pl.pallas_call(
        _mm_rowgroup_body,
        grid=(m // bm,),
        in_specs=[pl.BlockSpec((bm, k), lambda i: (i, 0)),
                  pl.BlockSpec((None, k, n), lambda i: (layer, 0, 0))],
        out_specs=pl.BlockSpec((bm // CMP_STRIDE, CMP_STRIDE * n), lambda i: (i, 0)),
        out_shape=jax.ShapeDtypeStruct((m // CMP_STRIDE, CMP_STRIDE * n), BF16),
        scratch_shapes=[pltpu.VMEM((n // HEAD_DIM, bm, HEAD_DIM), F32)],
        compiler_params=_params("parallel"),
        name=name,
    )(x, w)


def _compress(r, pe, w1, w2, layer):
    b, n, _ = r.shape
    return pl.pallas_call(
        functools.partial(_cmp_body, n=n),
        grid=(b,),
        in_specs=[pl.BlockSpec((None, n, CMP_STRIDE * 2 * KV_W), lambda i: (i, 0, 0)),
                  pl.BlockSpec((None, 2, 1, CMP_LEN * HEAD_DIM), lambda i: (layer, 0, 0, 0)),
                  pl.BlockSpec((None, 2, CMP_LEN * HEAD_DIM, HEAD_DIM), lambda i: (layer, 0, 0, 0)),
                  pl.BlockSpec((None, 2, HEAD_DIM, HEAD_DIM), lambda i: (layer, 0, 0, 0))],
        out_specs=pl.BlockSpec((None, 2, NSA_KV_GROUPS, n, HEAD_DIM), lambda i: (i, 0, 0, 0, 0)),
        out_shape=jax.ShapeDtypeStruct((b, 2, NSA_KV_GROUPS, n, HEAD_DIM), BF16),
        compiler_params=_params("parallel"),
        name="nsa_compress",
    )(r, pe, w1, w2)


def _masked_softmax(s, mask):
    m = jnp.max(s, axis=-1, keepdims=True)
    e = jnp.where(mask, jnp.exp(s - m), 0.0)
    l = jnp.sum(e, axis=-1, keepdims=True)
    return e / jnp.where(l > 0.0, l, 1.0)


def _nsa_body(q_ref, ks_ref, vs_ref, *rest, t_len, nb):
    nwin = WINDOW // Q_BLOCK + 1
    kw_refs, vw_refs = rest[:nwin], rest[nwin:2 * nwin]
    kc_ref, vc_ref, gate_ref, nz_ref, ov_ref, eh_ref, o_ref, m_ref, l_ref, acc_ref = rest[2 * nwin:]
    i = pl.program_id(1)
    t0 = i * Q_BLOCK
    rows = NSA_HPG * Q_BLOCK
    n_cmp = t_len // CMP_STRIDE
    n_sel = t_len // SEL_BLOCK
    kt = SEL_KEY_TILE
    streams = [(bb, g) for bb in range(nb) for g in range(NSA_KV_GROUPS)]

    def lanes(g, width):
        return slice(g * width, (g + 1) * width)

    rowi = lax.broadcasted_iota(jnp.int32, (rows, 1), 0)
    tcol = t0 + jnp.bitwise_and(rowi, Q_BLOCK - 1)
    head_slope = lambda g, h: 2.0 ** -(g * NSA_HPG + h + 1)
    slopes = [jnp.concatenate([jnp.full((Q_BLOCK, 1), head_slope(g, h), F32) for h in range(NSA_HPG)],
                              axis=0) for g in range(NSA_KV_GROUPS)]
    q4 = [jnp.concatenate([q_ref[bb, :, g * rows + h * HEAD_DIM:g * rows + (h + 1) * HEAD_DIM]
                           for h in range(NSA_HPG)], axis=0) for bb, g in streams]

    nidx = lax.broadcasted_iota(jnp.int32, (1, n_cmp), 1)
    d_c = tcol - (nidx * CMP_STRIDE + CMP_LEN - 1)
    mask_c = d_c >= 0
    d_cf = d_c.astype(F32)
    qk_c = [_dot_nt(q4[si], kc_ref[bb, g]) for si, (bb, g) in enumerate(streams)]
    qk_w = [_dot_nt(q4[si], jnp.concatenate([r[bb, :, lanes(g, HEAD_DIM)] for r in kw_refs], axis=0))
            for si, (bb, g) in enumerate(streams)]
    o_c, imp = [], []
    for si, (bb, g) in enumerate(streams):
        s = jnp.where(mask_c, qk_c[si] - slopes[g] * d_cf, NEG)
        p_c = _masked_softmax(s, mask_c)
        o_c.append(_dot(p_c.astype(BF16), vc_ref[bb, g]))
        p_sum = p_c[0:Q_BLOCK]
        for h in range(1, NSA_HPG):
            p_sum = p_sum + p_c[h * Q_BLOCK:(h + 1) * Q_BLOCK]
        p_hi = p_sum.astype(BF16)
        p_lo = (p_sum - p_hi.astype(F32)).astype(BF16)
        imp.append(_dot(p_hi, ov_ref[...]) + _dot(p_lo, ov_ref[...]))

    span = WINDOW + Q_BLOCK
    kpos_w = (t0 - WINDOW) + lax.broadcasted_iota(jnp.int32, (1, span), 1)
    d_w = tcol - kpos_w
    mask_w = (d_w >= 0) & (d_w < WINDOW) & (kpos_w >= 0)
    d_wf = d_w.astype(F32)
    o_w = []
    for si, (bb, g) in enumerate(streams):
        vw = jnp.concatenate([r[bb, :, lanes(g, HEAD_DIM)] for r in vw_refs], axis=0)
        s = jnp.where(mask_w, qk_w[si] - slopes[g] * d_wf, NEG)
        p_w = _masked_softmax(s, mask_w)
        o_w.append(_dot(p_w.astype(BF16), vw))

    jrow = lax.broadcasted_iota(jnp.int32, (n_sel, 1), 0)
    jf = jrow.astype(F32)
    tt = t0 + lax.broadcasted_iota(jnp.int32, (1, Q_BLOCK), 1)
    cur = tt // SEL_BLOCK
    forced = (jrow == 0) | (jrow == cur) | (jrow == cur - 1)
    future = jrow * SEL_BLOCK > tt
    work = [jnp.where(future, -1e9, jnp.where(forced, 1e9, im.T)) for im in imp]
    picked = [jnp.zeros((n_sel, Q_BLOCK), F32) for _ in streams]
    for _ in range(min(SEL_TOPK, n_sel)):
        for si in range(len(streams)):
            mx = jnp.max(work[si], axis=0, keepdims=True)
            first = jnp.min(jnp.where(work[si] == mx, jf, 1e9), axis=0, keepdims=True)
            pick = jf == first
            picked[si] = jnp.where(pick, 1.0, picked[si])
            work[si] = jnp.where(pick, -3e38, work[si])
    sel = [jnp.where(future, 0.0, pk).T > 0.5 for pk in picked]

    jidx = lax.broadcasted_iota(jnp.int32, (1, n_sel), 1)
    blk_off = jidx.astype(F32) * float(SEL_BLOCK)
    q_aug = []
    for si, (bb, g) in enumerate(streams):
        parts = []
        for h in range(NSA_HPG):
            bias_h = jnp.where(jidx == 0, head_slope(g, h),
                               jnp.where(sel[si], head_slope(g, h) * blk_off, NEG)).astype(BF16)
            parts.append(jnp.concatenate([q4[si][h * Q_BLOCK:(h + 1) * Q_BLOCK], bias_h], axis=1))
        q_aug.append(jnp.concatenate(parts, axis=0))
        m_ref[si] = jnp.full((rows, 1), NEG, F32)
        l_ref[si] = jnp.zeros((rows, 1), F32)
        acc_ref[si] = jnp.zeros((rows, HEAD_DIM), F32)

    def sel_tile(c, width, causal):
        k0 = pl.multiple_of(c * kt, kt)
        eh = eh_ref[pl.ds(k0, width), :]
        scores = []
        for si, (bb, g) in enumerate(streams):
            k_aug = jnp.concatenate([ks_ref[bb, pl.ds(k0, width), lanes(g, HEAD_DIM)], eh], axis=1)
            scores.append(_dot_nt(q_aug[si], k_aug))
        for si, (bb, g) in enumerate(streams):
            sc = scores[si]
            if causal:
                kpos = k0 + lax.broadcasted_iota(jnp.int32, (1, width), 1)
                sc = jnp.where(kpos <= tcol, sc, NEG)
            m_old = m_ref[si]
            m_new = jnp.maximum(m_old, jnp.max(sc, axis=-1, keepdims=True))
            alpha = jnp.exp(m_old - m_new)
            p = jnp.exp(sc - m_new)
            l_ref[si] = alpha * l_ref[si] + jnp.sum(p, axis=-1, keepdims=True)
            acc_ref[si] = alpha * acc_ref[si] + _dot(p.astype(BF16),
                                                     vs_ref[bb, pl.ds(k0, width), lanes(g, HEAD_DIM)])
            m_ref[si] = m_new

    c_last = t0 // kt

    def past_tile(c, carry):
        sel_tile(c, kt, False)
        return carry

    lax.fori_loop(0, c_last, past_tile, 0)
    needed = t0 - c_last * kt + Q_BLOCK
    lower = 0
    for width in SEL_DIAG_WIDTHS:
        @pl.when((needed > lower) & (needed <= width))
        def _(width=width):
            sel_tile(c_last, width, True)
        lower = width

    for si, (bb, g) in enumerate(streams):
        o_s = acc_ref[si] / l_ref[si]
        gt = jax.nn.sigmoid(gate_ref[bb, :, lanes(g, HEAD_DIM)])
        outs = []
        for h in range(NSA_HPG):
            r0, r1 = h * Q_BLOCK, (h + 1) * Q_BLOCK
            a = (gt[:, 3 * h:3 * h + 1] * o_c[si][r0:r1] + gt[:, 3 * h + 1:3 * h + 2] * o_s[r0:r1]
                 + gt[:, 3 * h + 2:3 * h + 3] * o_w[si][r0:r1])
            c0 = g * rows + h * HEAD_DIM
            outs.append(a * jax.nn.silu(nz_ref[bb, :, c0:c0 + HEAD_DIM]))
        o_ref[bb, :, lanes(g, rows)] = jnp.concatenate(outs, axis=1).astype(o_ref.dtype)


def _nsa_constants(t_len):
    n_cmp = t_len // CMP_STRIDE
    n_sel = t_len // SEL_BLOCK
    s = np.arange(n_cmp)[:, None] * CMP_STRIDE
    b = np.arange(n_sel)[None, :] * SEL_BLOCK
    ov = np.clip(np.minimum(s + CMP_LEN, b + SEL_BLOCK) - np.maximum(s, b), 0, None) / CMP_LEN
    ov[n_cmp - 1] = 0.0
    key_cols = (np.arange(t_len)[:, None] // SEL_BLOCK == np.arange(n_sel)[None, :]).astype(np.float32)
    key_cols[:, 0] = np.arange(t_len) % SEL_BLOCK
    return jnp.asarray(ov, BF16), jnp.asarray(key_cols, BF16)


def _nsa(p16, cmp_kv, gates, p32):
    b, t, _ = p16.shape
    nb = NSA_BATCH_PER_STEP
    n_cmp = t // CMP_STRIDE
    n_sel = t // SEL_BLOCK
    rows = NSA_HPG * Q_BLOCK
    nstreams = nb * NSA_KV_GROUPS
    ov, key_cols = _nsa_constants(t)
    kv_base = NSA_WIDTH // KV_W
    once = pl.Buffered(1)

    def kv_spec(section):
        return pl.BlockSpec((nb, t, KV_W), lambda bi, i: (bi, 0, kv_base + section), pipeline_mode=once)

    nwin = WINDOW // Q_BLOCK + 1

    def win_specs(section):
        return [pl.BlockSpec((nb, Q_BLOCK, KV_W),
                             lambda bi, i, k=k: (bi, jnp.maximum(i - (nwin - 1) + k, 0), kv_base + section))
                for k in range(nwin)]

    def cmp_spec(kv):
        return pl.BlockSpec((nb, None, NSA_KV_GROUPS, n_cmp, HEAD_DIM), lambda bi, i: (bi, kv, 0, 0, 0),
                            pipeline_mode=once)

    return pl.pallas_call(
        functools.partial(_nsa_body, t_len=t, nb=nb),
        grid=(b // nb, t // Q_BLOCK),
        in_specs=[
            pl.BlockSpec((nb, Q_BLOCK, NSA_WIDTH), lambda bi, i: (bi, i, 0)),
            kv_spec(0), kv_spec(1), *win_specs(2), *win_specs(3),
            cmp_spec(0), cmp_spec(1),
            pl.BlockSpec((nb, Q_BLOCK, NSA_KV_GROUPS * HEAD_DIM), lambda bi, i: (bi, i, 0)),
            pl.BlockSpec((nb, Q_BLOCK, NSA_WIDTH), lambda bi, i: (bi, i, 0)),
            pl.BlockSpec((n_cmp, n_sel), lambda bi, i: (0, 0), pipeline_mode=once),
            pl.BlockSpec((t, n_sel), lambda bi, i: (0, 0), pipeline_mode=once),
        ],
        out_specs=pl.BlockSpec((nb, Q_BLOCK, NSA_WIDTH), lambda bi, i: (bi, i, 0)),
        out_shape=jax.ShapeDtypeStruct((b, t, NSA_WIDTH), BF16),
        scratch_shapes=[pltpu.VMEM((nstreams, rows, 1), F32), pltpu.VMEM((nstreams, rows, 1), F32),
                        pltpu.VMEM((nstreams, rows, HEAD_DIM), F32)],
        compiler_params=_params("parallel", "arbitrary"),
        name="nsa_attention",
    )(*([p16] * (3 + 2 * nwin)), cmp_kv, cmp_kv, gates, p32, ov, key_cols)


def _hgrn_level_masks():
    t = np.arange(HG_CHUNK)[:, None]
    s = np.arange(HG_CHUNK)[None, :]
    masks = []
    w = HG_SUB
    while w < HG_CHUNK:
        masks.append((t // (2 * w) == s // (2 * w)) & (t % (2 * w) >= w) & (s % (2 * w) < w))
        w *= 2
    return jnp.asarray(np.stack(masks), F32)


def _hgrn_body(hq_ref, hf_ref, hi_ref, hz_ref, lb_ref, nw_ref, mk_ref, o_ref, st_ref, *, layer):
    c = HG_CHUNK

    @pl.when(pl.program_id(2) == 0)
    def _():
        st_ref[...] = jnp.zeros_like(st_ref)

    ri = lax.broadcasted_iota(jnp.int32, (c, 1), 0)
    ci = lax.broadcasted_iota(jnp.int32, (1, c), 1)
    tri = jnp.where(ci <= ri, 1.0, 0.0).astype(BF16)
    in_sub = jnp.bitwise_and(ri, HG_SUB - 1)

    heads = range(HG_HEADS_PER_STEP)
    lane = lambda hh: slice(hh * HG_DK, (hh + 1) * HG_DK)

    q, k, v, vb, g = [], [], [], [], []
    for hh in heads:
        ln = lane(hh)
        lbraw = lb_ref[:, ln]
        ex = jnp.exp(lbraw - jnp.max(lbraw, axis=0, keepdims=True))
        sm = ex / jnp.sum(ex, axis=0, keepdims=True)
        lb = jnp.zeros((1, HG_DK), F32)
        for r in range(1, layer + 1):
            lb = lb + sm[r:r + 1]
        hf = hf_ref[:, ln]
        q.append(jax.nn.silu(hq_ref[:, ln]))
        v.append(hi_ref[:, ln])
        vb.append(v[hh].astype(BF16))
        sg = jax.nn.sigmoid(hf)
        f = lb + (1.0 - lb) * sg
        k.append((1.0 - lb) * (1.0 - sg))
        lf = jnp.log(f) * LOG2E
        lf1 = lf.astype(BF16)
        rem = lf - lf1.astype(F32)
        lf2 = rem.astype(BF16)
        lf3 = (rem - lf2.astype(F32)).astype(BF16)
        g.append(_dot(tri, lf1) + _dot(tri, lf2) + _dot(tri, lf3))

    st = [st_ref[hh] for hh in heads]
    o = [_dot_nt((q[hh] * jnp.exp2(g[hh])).astype(BF16), st[hh].astype(BF16)) for hh in heads]

    a_off = [jnp.zeros((c, c), F32) for _ in heads]
    w = HG_SUB
    level = 0
    while w < c:
        for hh in heads:
            gref = jnp.concatenate(
                [jnp.broadcast_to(g[hh][base + w - 1:base + w], (2 * w, HG_DK)) for base in range(0, c, 2 * w)],
                axis=0)
            e = jnp.exp2(-jnp.abs(g[hh] - gref))
            a_off[hh] = a_off[hh] + mk_ref[level] * _dot_nt((q[hh] * e).astype(BF16), (k[hh] * e).astype(BF16))
        w *= 2
        level += 1
    for hh in heads:
        o[hh] = o[hh] + _dot(a_off[hh].astype(BF16), vb[hh])

    for hh in heads:
        ln = lane(hh)
        oh = o[hh] + jnp.sum(q[hh] * k[hh], axis=-1, keepdims=True) * v[hh]
        for d in range(1, HG_SUB):
            live = in_sub >= d
            dec = jnp.exp2(jnp.where(live, g[hh] - pltpu.roll(g[hh], d, 0), 0.0))
            a_col = jnp.sum(q[hh] * (pltpu.roll(k[hh], d, 0) * dec), axis=-1, keepdims=True)
            oh = oh + jnp.where(live, a_col, 0.0) * pltpu.roll(v[hh], d, 0)
        oh = oh * lax.rsqrt(jnp.mean(oh * oh, axis=-1, keepdims=True) + LN_EPS)
        o_ref[:, ln] = (oh * nw_ref[:, ln] * jax.nn.silu(hz_ref[:, ln])).astype(o_ref.dtype)

        g_last = g[hh][c - 1:c]
        kd = (k[hh] * jnp.exp2(g_last - g[hh])).astype(BF16)
        st_ref[hh] = st[hh] * jnp.exp2(g_last) + _dot_tn(vb[hh], kd)


def _hgrn(p32, lb_raw, norm_w, layer):
    b, t, _ = p32.shape
    hps = HG_HEADS_PER_STEP
    wide = hps * HG_DK
    base = NSA_WIDTH // wide
    per_section = HG_WIDTH // wide
    masks = _hgrn_level_masks()

    def col_spec(section):
        return pl.BlockSpec((None, HG_CHUNK, wide), lambda bi, h, c: (bi, c, base + section * per_section + h))

    return pl.pallas_call(
        functools.partial(_hgrn_body, layer=layer),
        grid=(b, HG_HEADS // hps, t // HG_CHUNK),
        in_specs=[col_spec(0), col_spec(1), col_spec(2), col_spec(3),
                  pl.BlockSpec((N_EVEN, wide), lambda bi, h, c: (0, h)),
                  pl.BlockSpec((None, 1, wide), lambda bi, h, c: (layer, 0, h)),
                  pl.BlockSpec(masks.shape, lambda bi, h, c: (0, 0, 0))],
        out_specs=pl.BlockSpec((None, HG_CHUNK, wide), lambda bi, h, c: (bi, c, h)),
        out_shape=jax.ShapeDtypeStruct((b, t, HG_WIDTH), BF16),
        scratch_shapes=[pltpu.VMEM((hps, HG_DV, HG_DK), F32)],
        compiler_params=_params("parallel", "parallel", "arbitrary"),
        name="hgrn2",
    )(p32, p32, p32, p32, lb_raw, norm_w, masks)


def _glu_body(x_ref, wa_ref, wb_ref, wz_ref, u_ref, sz_ref):
    x = x_ref[...].astype(BF16)
    u_ref[...] = _dot(x, wa_ref[...]) * jax.nn.sigmoid(_dot(x, wb_ref[...]))
    sz_ref[...] = jax.nn.silu(_dot(x, wz_ref[...]))


def _glu_proj(x, w, layer, bm, bn):
    m, k = x.shape
    nb = CONV_CH // bn

    def w_spec(part):
        return pl.BlockSpec((None, k, bn), lambda i, j: (layer, 0, part * nb + j))

    out = jax.ShapeDtypeStruct((m, CONV_CH), F32)
    return pl.pallas_call(
        _glu_body,
        grid=(m // bm, nb),
        in_specs=[pl.BlockSpec((bm, k), lambda i, j: (i, 0)), w_spec(0), w_spec(1), w_spec(2)],
        out_specs=[pl.BlockSpec((bm, bn), lambda i, j: (i, j))] * 2,
        out_shape=[out, out],
        compiler_params=_params("parallel", "arbitrary"),
        name="conv_glu_proj",
    )(x, w, w, w)


def _conv_body(uc_ref, up_ref, sz_ref, w_ref, cb_ref, g_ref, b_ref, o_ref, buf_ref, c_ref):
    first_tile = pl.program_id(1) == 0
    buf_ref[0:CONV_HALO, :] = jnp.where(first_tile, 0.0, up_ref[...])
    buf_ref[CONV_HALO:CONV_HALO + CONV_ROWS, :] = uc_ref[...]
    lead = CONV_HALO - (CONV_K - 1)
    buf_rows = CONV_HALO + CONV_ROWS
    for cs in range(0, CONV_CH, CONV_STRIP):
        x = buf_ref[:, cs:cs + CONV_STRIP]
        acc = jnp.broadcast_to(cb_ref[:, cs:cs + CONV_STRIP], (CONV_ROWS, CONV_STRIP))
        for phase in range(SUBLANES):
            xr = x if phase == 0 else pltpu.roll(x, buf_rows - phase, 0)
            for k in range(CONV_K):
                if (lead + k) % SUBLANES == phase:
                    a = lead + k - phase
                    acc = acc + xr[a:a + CONV_ROWS] * w_ref[k:k + 1, cs:cs + CONV_STRIP]
        c_ref[:, cs:cs + CONV_STRIP] = acc
    c = c_ref[...]
    mu = jnp.mean(c, axis=-1, keepdims=True)
    d = c - mu
    var = jnp.mean(d * d, axis=-1, keepdims=True)
    cn = d * lax.rsqrt(var + LN_EPS) * g_ref[...] + b_ref[...]
    o_ref[...] = (jax.nn.silu(cn) * sz_ref[...]).astype(o_ref.dtype)


def _conv_module(u, sz, conv_w, conv_b, ln_g, ln_b, layer):
    b, t, ch = u.shape
    halo_per_tile = CONV_ROWS // CONV_HALO
    row = lambda: pl.BlockSpec((None, 1, ch), lambda bi, i: (layer, 0, 0))
    tile = lambda: pl.BlockSpec((None, CONV_ROWS, ch), lambda bi, i: (bi, i, 0))
    return pl.pallas_call(
        _conv_body,
        grid=(b, t // CONV_ROWS),
        in_specs=[tile(),
                  pl.BlockSpec((None, CONV_HALO, ch),
                               lambda bi, i: (bi, jnp.maximum(i * halo_per_tile - 1, 0), 0)),
                  tile(),
                  pl.BlockSpec((None, CONV_HALO, ch), lambda bi, i: (layer, 0, 0)),
                  row(), row(), row()],
        out_specs=tile(),
        out_shape=jax.ShapeDtypeStruct((b, t, ch), BF16),
        scratch_shapes=[pltpu.VMEM((CONV_HALO + CONV_ROWS, ch), F32), pltpu.VMEM((CONV_ROWS, ch), F32)],
        compiler_params=_params("parallel", "arbitrary"),
        name="conv_module",
    )(u, u, sz, conv_w, conv_b, ln_g, ln_b)


def _post_body(*refs, nparts):
    y_refs = refs[:nparts]
    w_refs = refs[nparts:2 * nparts]
    x_ref, p_ref, plw_ref, gw_ref, lg_ref, lb_ref, o32_ref, o16_ref = refs[2 * nparts:]
    y = _dot(y_refs[0][...], w_refs[0][...])
    for r in range(1, nparts):
        y = y + _dot(y_refs[r][...], w_refs[r][...])
    ple = _dot(p_ref[...].astype(BF16), plw_ref[...])
    h = DEEPNORM_ALPHA * x_ref[...] + y
    mu = jnp.mean(h, axis=-1, keepdims=True)
    d = h - mu
    var = jnp.mean(d * d, axis=-1, keepdims=True)
    x1 = d * lax.rsqrt(var + LN_EPS) * lg_ref[...] + lb_ref[...]
    gate = jax.nn.sigmoid(_dot(x1.astype(BF16), gw_ref[...]))
    out = x1 + ple * gate
    o32_ref[...] = out
    o16_ref[...] = out.astype(BF16)


def _post(y_parts, w_out, w_layer, x, p, layer, ple_w, gate_w, ln_g, ln_b, bm):
    m, d = x.shape
    nparts = len(y_parts)
    kp = y_parts[0].shape[1]
    once = pl.Buffered(1)
    in_specs = ([pl.BlockSpec((bm, kp), lambda i: (i, 0)) for _ in y_parts]
                + [pl.BlockSpec((None, kp, d), lambda i, r=r: (w_layer, r, 0), pipeline_mode=once)
                   for r in range(nparts)]
                + [pl.BlockSpec((bm, d), lambda i: (i, 0)),
                   pl.BlockSpec((None, bm, PLE_DIM), lambda i: (layer, i, 0)),
                   pl.BlockSpec((None, PLE_DIM, d), lambda i: (layer, 0, 0), pipeline_mode=once),
                   pl.BlockSpec((None, d, d), lambda i: (layer, 0, 0), pipeline_mode=once),
                   pl.BlockSpec((None, 1, d), lambda i: (layer, 0, 0), pipeline_mode=once),
                   pl.BlockSpec((None, 1, d), lambda i: (layer, 0, 0), pipeline_mode=once)])
    return pl.pallas_call(
        functools.partial(_post_body, nparts=nparts),
        grid=(m // bm,),
        in_specs=in_specs,
        out_specs=[pl.BlockSpec((bm, d), lambda i: (i, 0))] * 2,
        out_shape=[jax.ShapeDtypeStruct((m, d), F32), jax.ShapeDtypeStruct((m, d), BF16)],
        compiler_params=_params("parallel"),
        name="outproj_deepnorm_ple",
    )(*y_parts, *([w_out] * nparts), x, p, ple_w, gate_w, ln_g, ln_b)


EVEN_SECTIONS = np.cumsum([0, NSA_WIDTH, KV_W, KV_W, KV_W, KV_W, KV_W, KV_W, NSA_HEADS * 3, NSA_WIDTH,
                           HG_WIDTH, HG_WIDTH, HG_WIDTH, HG_WIDTH])
WPREP_COLS = 256
WPREP_PIECE = 128


def _even_weights_body(wt_ref, w16_ref, wc_ref, w32_ref, wg_ref):
    o = EVEN_SECTIONS

    def move(dst_ref, dst_lo, src_lo, src_hi, scale=None):
        for r in range(src_lo, src_hi, WPREP_PIECE):
            piece = wt_ref[r:r + WPREP_PIECE, :]
            if scale is not None:
                piece = piece * scale
            c = dst_lo + r - src_lo
            dst_ref[:, c:c + WPREP_PIECE] = piece.T.astype(BF16)

    move(w16_ref, 0, o[0], o[1], HEAD_DIM ** -0.5)
    move(w16_ref, NSA_WIDTH, o[3], o[7])
    move(wc_ref, 0, o[1], o[3])
    move(w32_ref, 0, o[8], o[13])
    per_group = NSA_HPG * 3
    gt = wt_ref[o[7]:o[7] + WPREP_PIECE, :].T
    lane = lax.broadcasted_iota(jnp.int32, gt.shape, 1)
    for g in range(NSA_KV_GROUPS):
        shifted = gt if g == 0 else pltpu.roll(gt, WPREP_PIECE - g * per_group, 1)
        wg_ref[:, g * HEAD_DIM:(g + 1) * HEAD_DIM] = jnp.where(lane < per_group, shifted, 0.0).astype(BF16)


def _even_weights(w_in):
    nl, k, n = w_in.shape
    widths = (NSA_WIDTH + 4 * KV_W, 2 * KV_W, NSA_WIDTH + 4 * HG_WIDTH, NSA_KV_GROUPS * HEAD_DIM)
    return pl.pallas_call(
        _even_weights_body,
        grid=(nl, k // WPREP_COLS),
        in_specs=[pl.BlockSpec((None, n, WPREP_COLS), lambda l, i: (l, 0, i))],
        out_specs=[pl.BlockSpec((None, WPREP_COLS, wd), lambda l, i: (l, i, 0)) for wd in widths],
        out_shape=[jax.ShapeDtypeStruct((nl, k, wd), BF16) for wd in widths],
        compiler_params=_params("parallel", "parallel"),
        name="even_weight_prep",
    )(jnp.swapaxes(w_in, 1, 2))


def _even_layer(x16, b, t, weights, cmp_weights, hg_norm, lb_raw, layer):
    w16, wc, w32, wg = weights
    if x16.dtype == BF16:
        p16 = _matmul(x16, w16, layer, BF16, 1024, 1024, "even_proj_qkv")
    else:
        p16, x16 = _matmul(x16, w16, layer, BF16, 1024, 1024, "even_proj_qkv_cast", emit_x16=True)
    pc = _matmul_rowgroup(x16, wc, layer, 1024, "even_proj_cmp")
    p32 = _matmul(x16, w32, layer, F32, 1024, 1024, "even_proj_gates")
    gates = _matmul(x16, wg, layer, F32, 1024, NSA_KV_GROUPS * HEAD_DIM, "even_proj_nsa_gates")
    cmp_kv = _compress(pc.reshape(b, t // CMP_STRIDE, -1), *cmp_weights, layer)
    p32 = p32.reshape(b, t, -1)
    ya = _nsa(p16.reshape(b, t, -1), cmp_kv, gates.reshape(b, t, -1), p32)
    yo = _hgrn(p32, lb_raw, hg_norm, layer)
    return ya.reshape(b * t, NSA_WIDTH), yo.reshape(b * t, HG_WIDTH)


def _odd_layer(x16, b, t, w_in16, conv_w, conv_b, ln_g, ln_b, layer):
    u, sz = _glu_proj(x16, w_in16, layer, 1024, 512)
    y = _conv_module(u.reshape(b, t, CONV_CH), sz.reshape(b, t, CONV_CH), conv_w, conv_b, ln_g, ln_b, layer)
    return y.reshape(b * t, CONV_CH)


def kernel(x, p, ev_w_in, ev_cmp_pe_k, ev_cmp_w1_k, ev_cmp_w2_k, ev_cmp_pe_v, ev_cmp_w1_v,
           ev_cmp_w2_v, ev_hg_norm, hgrn_lb, ev_w_out, od_w_in, od_conv_w, od_conv_b, od_ln_g,
           od_ln_b, od_w_out, post_ln_g, post_ln_b, ple_w, ple_gate_w):
    b, t, d = x.shape
    x32 = x.reshape(b * t, d)
    x16 = x32
    even_w = _even_weights(ev_w_in)
    n_even = ev_w_in.shape[0]
    cmp_w = (jnp.stack([ev_cmp_pe_k, ev_cmp_pe_v], axis=1).reshape(n_even, 2, 1, CMP_LEN * HEAD_DIM),
             jnp.stack([ev_cmp_w1_k, ev_cmp_w1_v], axis=1).astype(BF16),
             jnp.stack([ev_cmp_w2_k, ev_cmp_w2_v], axis=1).astype(BF16))
    hg_norm = ev_hg_norm.reshape(n_even, 1, HG_WIDTH)
    ev_w_out16 = ev_w_out.astype(BF16)
    od_w_in16 = od_w_in.astype(BF16)
    od_w_out16 = od_w_out.astype(BF16)
    n_odd = od_w_in.shape[0]
    conv_w = jnp.pad(od_conv_w, ((0, 0), (0, CONV_HALO - CONV_K), (0, 0)))
    conv_b = od_conv_b.reshape(n_odd, 1, CONV_CH)
    conv_g = od_ln_g.reshape(n_odd, 1, CONV_CH)
    conv_beta = od_ln_b.reshape(n_odd, 1, CONV_CH)
    p_all = p.reshape(DEPTH, b * t, PLE_DIM)
    ple_w16 = ple_w.astype(BF16)
    gate_w16 = ple_gate_w.astype(BF16)
    ln_g = post_ln_g.reshape(DEPTH, 1, d)
    ln_b = post_ln_b.reshape(DEPTH, 1, d)
    for i in range(DEPTH):
        j = i // 2
        if i % 2 == 0:
            y_parts = list(_even_layer(x16, b, t, even_w, cmp_w, hg_norm, hgrn_lb, j))
            w_out = ev_w_out16
        else:
            y_parts = [_odd_layer(x16, b, t, od_w_in16, conv_w, conv_b, conv_g, conv_beta, j)]
            w_out = od_w_out16
        x32, x16 = _post(y_parts, w_out, j, x32, p_all, i, ple_w16, gate_w16, ln_g, ln_b, 256)
    return x32.reshape(b, t, d)
```
